```python
import math
import jax, jax.numpy as jnp
from jax import lax
import numpy as np

D_MODEL = 4096
BATCH = 1
SEQ = 8192
DEPTH = 1
DEC_BATCH = 128
DEC_SEQ = 8
PAST_LEN = 2048
PAGE_SIZE = 128

GLA_HEADS = 8
GLA_DK = D_MODEL // 32
GLA_DV = D_MODEL // 16
GLA_KEY = GLA_HEADS * GLA_DK
GLA_WIDTH = GLA_HEADS * GLA_DV
GLA_RANK = 16
GLA_TAU = 16.0
GLA_CHUNK = 64
DIFF_HEADS = 8
DIFF_DK = D_MODEL // 32
DIFF_DV = 2 * DIFF_DK
DIFF_QK = DIFF_HEADS * 2 * DIFF_DK
DIFF_WIDTH = DIFF_HEADS * DIFF_DV
Q_BLOCK = 128
D_FF = 4 * D_MODEL
EPS = 1e-6
IN_SIZES = [GLA_KEY, GLA_KEY, GLA_WIDTH, GLA_WIDTH, GLA_RANK,
            DIFF_QK, DIFF_QK, DIFF_WIDTH, D_MODEL, D_MODEL]
N_IN = sum(IN_SIZES)
IN_SPLITS = [int(o) for o in np.cumsum(IN_SIZES)[:-1]]

kernel_name = "hybrid_gla_diffattn_alibi_step"


def rmsnorm(x, g):
    xf = x.astype(jnp.float32)
    y = xf * lax.rsqrt(jnp.mean(xf * xf, axis=-1, keepdims=True) + EPS) * g.astype(jnp.float32)
    return y.astype(x.dtype)


def head_rmsnorm(o, g, n_heads, dh):
    o = o * lax.rsqrt(jnp.mean(o * o, axis=-1, keepdims=True) + EPS)
    return o * g.astype(jnp.float32).reshape(n_heads, dh)


def alibi_slopes():
    h = jnp.arange(1, DIFF_HEADS + 1, dtype=jnp.float32)
    return jnp.exp2(-8.0 * h / DIFF_HEADS)


def mixer_inputs(x, norm_g, w_in, w_a2, b_a):
    B, T = x.shape[0], x.shape[1]
    h = rmsnorm(x, norm_g)
    z = h @ w_in
    gq, gk, gv, gr, g_lr, dq, dk, dv, gate_a, gate_b = jnp.split(z, IN_SPLITS, axis=-1)
    f32 = jnp.float32
    gla_q = gq.reshape(B, T, GLA_HEADS, GLA_DK).astype(f32) * (GLA_DK ** -0.5)
    gla_k = gk.reshape(B, T, GLA_HEADS, GLA_DK).astype(f32)
    gla_v = gv.reshape(B, T, GLA_HEADS, GLA_DV).astype(f32)
    log_a = jax.nn.log_sigmoid((g_lr @ w_a2 + b_a).astype(f32)) / GLA_TAU
    log_a = log_a.reshape(B, T, GLA_HEADS, GLA_DK)
    diff_q = dq.reshape(B, T, DIFF_HEADS, 2, DIFF_DK)
    diff_k = dk.reshape(B, T, DIFF_HEADS, 2, DIFF_DK)
    diff_v = dv.reshape(B, T, DIFF_HEADS, DIFF_DV)
    return gla_q, gla_k, gla_v, log_a, gr, diff_q, diff_k, diff_v, gate_a, gate_b


def gla_recurrence(q, k, v, log_a, s0, chunk):
    B, T, H, dk = q.shape
    dv = v.shape[-1]
    n = T // chunk

    def blocks(a):
        return a.reshape(B, n, chunk, H, a.shape[-1]).swapaxes(0, 1)

    causal = jnp.tril(jnp.ones((chunk, chunk), dtype=bool))[None, :, :, None, None]

    def step(S, inp):
        qc, kc, vc, ac = inp
        b = jnp.cumsum(ac, axis=1)
        o_inter = jnp.einsum('bthk,bhkv->bthv', qc * jnp.exp(b), S)
        decay = jnp.exp(jnp.where(causal, b[:, :, None] - b[:, None, :], -jnp.inf))
        att = jnp.einsum('bthk,bshk,btshk->bhts', qc, kc, decay)
        o_intra = jnp.einsum('bhts,bshv->bthv', att, vc)
        b_last = b[:, -1]
        k_dec = kc * jnp.exp(b_last[:, None] - b)
        S_new = jnp.exp(b_last)[..., None] * S + jnp.einsum('bshk,bshv->bhkv', k_dec, vc)
        return S_new, o_inter + o_intra

    s_fin, o = lax.scan(step, s0, (blocks(q), blocks(k), blocks(v), blocks(log_a)))
    return o.swapaxes(0, 1).reshape(B, T, H, dv), s_fin


def gla_output(o, r, g, dtype):
    B, T = o.shape[0], o.shape[1]
    o = head_rmsnorm(o, g, GLA_HEADS, GLA_DV).reshape(B, T, GLA_WIDTH)
    return (o * jax.nn.silu(r.astype(jnp.float32))).astype(dtype)


def diff_weights(s, dist, slopes, lam):
    s = s.astype(jnp.float32) * (DIFF_DK ** -0.5) - slopes[:, None, None] * dist.astype(jnp.float32)
    s = jnp.where(dist >= 0, s, -jnp.inf)
    p = jax.nn.softmax(s, axis=-1)
    return p[:, 0] - lam * p[:, 1]


def diff_attn_prompt(q, k, v, lam, slopes):
    B, T = q.shape[0], q.shape[1]
    nb = T // Q_BLOCK
    pos_k = jnp.arange(T)
    vf = v.astype(jnp.float32)
    qb = q.reshape(B, nb, Q_BLOCK, DIFF_HEADS, 2, DIFF_DK).swapaxes(0, 1)

    def block(args):
        qi, i = args
        pos_q = i * Q_BLOCK + jnp.arange(Q_BLOCK)
        s = jnp.einsum('bqhmd,bkhmd->bmhqk', qi, k)
        w = diff_weights(s, pos_q[:, None] - pos_k[None, :], slopes, lam)
        return jnp.einsum('bhqk,bkhd->bqhd', w, vf)

    o = lax.map(block, (qb, jnp.arange(nb)))
    return o.swapaxes(0, 1).reshape(B, T, DIFF_HEADS, DIFF_DV)


def diff_attn_sample(q, k_new, v_new, k_past, v_past, lam, slopes):
    Tn = q.shape[1]
    P = k_past.shape[1]
    pos_q = P + jnp.arange(Tn)
    pos_k = jnp.concatenate([jnp.arange(P), P + jnp.arange(Tn)])
    s = jnp.concatenate([jnp.einsum('bqhmd,bkhmd->bmhqk', q, k_past),
                         jnp.einsum('bqhmd,bkhmd->bmhqk', q, k_new)], axis=-1)
    w = diff_weights(s, pos_q[:, None] - pos_k[None, :], slopes, lam)
    return (jnp.einsum('bhqk,bkhd->bqhd', w[..., :P], v_past.astype(jnp.float32))
            + jnp.einsum('bhqk,bkhd->bqhd', w[..., P:], v_new.astype(jnp.float32)))


def diff_output(o, g, lam_init, dtype):
    B, T = o.shape[0], o.shape[1]
    o = head_rmsnorm(o, g, DIFF_HEADS, DIFF_DV) * (1.0 - lam_init)
    return o.reshape(B, T, DIFF_WIDTH).astype(dtype)


def merge_and_mlp(x, y_a, y_b, gate_a, gate_b, w_pa, w_pb, w_o, norm_mlp, w_up, w_down):
    u = jax.nn.sigmoid(gate_a) * (y_a @ w_pa) + jax.nn.sigmoid(gate_b) * (y_b @ w_pb)
    x = x + u @ w_o
    h = rmsnorm(x, norm_mlp)
    return x + jnp.square(jax.nn.relu(h @ w_up)) @ w_down


def setup_inputs(seed: int = 0) -> dict:
    key = jax.random.key(seed)
    ks = jax.random.split(key, 24)
    f32 = jnp.float32
    n_pages = PAST_LEN // PAGE_SIZE
    n_pool = (DEC_BATCH * n_pages * 5) // 4

    def nrm(k, shape, scale):
        return jax.random.normal(k, shape, f32) * scale

    def gain(k, shape):
        return 1.0 + 0.01 * jax.random.normal(k, shape, f32)

    perm = jax.random.permutation(ks[0], n_pool)
    page_table = perm[:DEC_BATCH * n_pages].reshape(DEC_BATCH, n_pages).astype(jnp.int32)
    return {
        'x_prompt': nrm(ks[1], (BATCH, SEQ, D_MODEL), 1.0),
        'x_sample': nrm(ks[2], (DEC_BATCH, DEC_SEQ, D_MODEL), 1.0),
        'cache_k': nrm(ks[3], (DEPTH, n_pool, PAGE_SIZE, DIFF_HEADS, 2 * DIFF_DK), 1.0),
        'cache_v': nrm(ks[4], (DEPTH, n_pool, PAGE_SIZE, DIFF_HEADS, DIFF_DV), 1.0),
        'state_gla': nrm(ks[5], (DEPTH, DEC_BATCH, GLA_HEADS, GLA_DK, GLA_DV), 1.0),
        'page_table': page_table,
        'norm_mix': gain(ks[6], (DEPTH, D_MODEL)),
        'w_in': nrm(ks[7], (DEPTH, D_MODEL, N_IN), D_MODEL ** -0.5),
        'w_alpha2': nrm(ks[8], (DEPTH, GLA_RANK, GLA_KEY), GLA_RANK ** -0.5),
        'b_alpha': nrm(ks[9], (DEPTH, GLA_KEY), 0.1),
        'gla_norm': gain(ks[10], (DEPTH, GLA_WIDTH)),
        'lambda_q1': nrm(ks[11], (DEPTH, DIFF_DK), 0.1),
        'lambda_k1': nrm(ks[12], (DEPTH, DIFF_DK), 0.1),
        'lambda_q2': nrm(ks[13], (DEPTH, DIFF_DK), 0.1),
        'lambda_k2': nrm(ks[14], (DEPTH, DIFF_DK), 0.1),
        'diff_norm': gain(ks[15], (DEPTH, DIFF_WIDTH)),
        'w_proj_a': nrm(ks[16], (DEPTH, GLA_WIDTH, D_MODEL), GLA_WIDTH ** -0.5),
        'w_proj_b': nrm(ks[17], (DEPTH, DIFF_WIDTH, D_MODEL), DIFF_WIDTH ** -0.5),
        'w_o': nrm(ks[18], (DEPTH, D_MODEL, D_MODEL), D_MODEL ** -0.5),
        'norm_mlp': gain(ks[19], (DEPTH, D_MODEL)),
        'w_up': nrm(ks[20], (DEPTH, D_MODEL, D_FF), D_MODEL ** -0.5),
        'w_down': nrm(ks[21], (DEPTH, D_FF, D_MODEL), D_FF ** -0.5),
        'final_norm': gain(ks[22], (D_MODEL,)),
    }


def reference(x_prompt, x_sample, cache_k, cache_v, state_gla, page_table,
              norm_mix, w_in, w_alpha2, b_alpha, gla_norm,
              lambda_q1, lambda_k1, lambda_q2, lambda_k2, diff_norm,
              w_proj_a, w_proj_b, w_o, norm_mlp, w_up, w_down, final_norm):
    slopes = alibi_slopes()
    Bp, Tp = x_prompt.shape[0], x_prompt.shape[1]
    Bs, Ts = x_sample.shape[0], x_sample.shape[1]
    n_pages = page_table.shape[1]
    past = n_pages * PAGE_SIZE
    xp, xs = x_prompt, x_sample
    kp_l, vp_l, sp_l, ks_l, vs_l, ss_l = [], [], [], [], [], []
    for l in range(DEPTH):
        lam_init = 0.8 - 0.6 * math.exp(-0.3 * l)
        lam = (jnp.exp(jnp.sum(lambda_q1[l].astype(jnp.float32) * lambda_k1[l].astype(jnp.float32)))
               - jnp.exp(jnp.sum(lambda_q2[l].astype(jnp.float32) * lambda_k2[l].astype(jnp.float32)))
               + lam_init)

        gq, gk, gv, ga, gr, dq, dk, dv, gate_a, gate_b = mixer_inputs(xp, norm_mix[l], w_in[l], w_alpha2[l], b_alpha[l])
        s0 = jnp.zeros((Bp, GLA_HEADS, GLA_DK, GLA_DV), jnp.float32)
        o_a, s_fin_p = gla_recurrence(gq, gk, gv, ga, s0, min(GLA_CHUNK, Tp))
        y_a = gla_output(o_a, gr, gla_norm[l], xp.dtype)
        o_b = diff_attn_prompt(dq, dk, dv, lam, slopes)
        y_b = diff_output(o_b, diff_norm[l], lam_init, xp.dtype)
        xp = merge_and_mlp(xp, y_a, y_b, gate_a, gate_b, w_proj_a[l], w_proj_b[l], w_o[l], norm_mlp[l], w_up[l], w_down[l])
        kp_l.append(dk.reshape(Bp, Tp, DIFF_HEADS, 2 * DIFF_DK))
        vp_l.append(dv)
        sp_l.append(s_fin_p.astype(state_gla.dtype))

        gq, gk, gv, ga, gr, dq, dk, dv, gate_a, gate_b = mixer_inputs(xs, norm_mix[l], w_in[l], w_alpha2[l], b_alpha[l])
        o_a, s_fin_s = gla_recurrence(gq, gk, gv, ga, state_gla[l].astype(jnp.float32), Ts)
        y_a = gla_output(o_a, gr, gla_norm[l], xs.dtype)
        k_past = cache_k[l][page_table].reshape(Bs, past, DIFF_HEADS, 2, DIFF_DK)
        v_past = cache_v[l][page_table].reshape(Bs, past, DIFF_HEADS, DIFF_DV)
        o_b = diff_attn_sample(dq, dk, dv, k_past, v_past, lam, slopes)
        y_b = diff_output(o_b, diff_norm[l], lam_init, xs.dtype)
        xs = merge_and_mlp(xs, y_a, y_b, gate_a, gate_b, w_proj_a[l], w_proj_b[l], w_o[l], norm_mlp[l], w_up[l], w_down[l])
        ks_l.append(dk.reshape(Bs, Ts, DIFF_HEADS, 2 * DIFF_DK))
        vs_l.append(dv)
        ss_l.append(s_fin_s.astype(state_gla.dtype))

    y_prompt = rmsnorm(xp, final_norm)
    y_sample = rmsnorm(xs, final_norm)
    return (y_prompt, y_sample, jnp.stack(kp_l), jnp.stack(vp_l), jnp.stack(sp_l),
            jnp.stack(ks_l), jnp.stack(vs_l), jnp.stack(ss_l))
```

```python
import functools
import math

import jax
import jax.numpy as jnp
from jax import lax
from jax.experimental import pallas as pl
from jax.experimental.pallas import tpu as pltpu

F32 = jnp.float32
BF16 = jnp.bfloat16

D_MODEL = 4096
PAGE_SIZE = 128
GLA_HEADS = 8
GLA_DK = 128
GLA_DV = 256
GLA_KEY = GLA_HEADS * GLA_DK
GLA_WIDTH = GLA_HEADS * GLA_DV
GLA_RANK = 16
GLA_TAU = 16.0
DIFF_HEADS = 8
DIFF_DK = 128
DIFF_DV = 256
DIFF_QK = DIFF_HEADS * 2 * DIFF_DK
DIFF_WIDTH = DIFF_HEADS * DIFF_DV
EPS = 1e-6

Z_GQ = 0
Z_GK = Z_GQ + GLA_KEY
Z_GV = Z_GK + GLA_KEY
Z_GR = Z_GV + GLA_WIDTH
Z_DQ = Z_GR + GLA_WIDTH
Z_DK = Z_DQ + DIFF_QK
Z_DV = Z_DK + DIFF_QK
Z_GA = Z_DV + DIFF_WIDTH
Z_GB = Z_GA + D_MODEL
Z_COLS = Z_GB + D_MODEL
LR_START = 2 * GLA_KEY + 2 * GLA_WIDTH

V7X_LANES = 128
V7X_SUBLANES = 8
V7X_VMEM_LIMIT_BYTES = 56 * 1024 * 1024

GLA_ROWS = 128
GLA_SUB = 16
SAMPLE_T = 8
NEG_BIG = -1e30


def _pick_block(n, pref, mult=V7X_LANES):
    b = (min(pref, n) // mult) * mult
    while b > mult and n % b:
        b -= mult
    assert b > 0 and n % b == 0, (n, pref, mult)
    return b


def _cparams(sem):
    return pltpu.CompilerParams(dimension_semantics=sem, vmem_limit_bytes=V7X_VMEM_LIMIT_BYTES)


def _dot(a, b):
    return jnp.dot(a, b, preferred_element_type=F32)


def _dot_nt(a, b):
    return lax.dot_general(a, b, (((1,), (1,)), ((), ())), preferred_element_type=F32)


def _norm_lr_kernel(x_ref, g_ref, wlr_ref, wa2_ref, ba_ref, h_ref, la_ref):
    x = x_ref[...]
    ms = jnp.mean(x * x, axis=-1, keepdims=True)
    h = (x * lax.rsqrt(ms + EPS) * g_ref[...]).astype(BF16)
    h_ref[...] = h
    g_lr = _dot(h, wlr_ref[...])
    xa = _dot(g_lr.astype(BF16), wa2_ref[...]) + ba_ref[...]
    la_ref[...] = jax.nn.log_sigmoid(xa) * (1.0 / GLA_TAU)


def _norm_lr(x, g, w_lr, w_a2, b_a):
    m = x.shape[0]
    bm = _pick_block(m, 256)
    return pl.pallas_call(
        _norm_lr_kernel,
        grid=(m // bm,),
        in_specs=[
            pl.BlockSpec((bm, D_MODEL), lambda i: (i, 0)),
            pl.BlockSpec((1, D_MODEL), lambda i: (0, 0)),
            pl.BlockSpec((D_MODEL, V7X_LANES), lambda i: (0, 0)),
            pl.BlockSpec((V7X_LANES, GLA_KEY), lambda i: (0, 0)),
            pl.BlockSpec((1, GLA_KEY), lambda i: (0, 0)),
        ],
        out_specs=[
            pl.BlockSpec((bm, D_MODEL), lambda i: (i, 0)),
            pl.BlockSpec((bm, GLA_KEY), lambda i: (i, 0)),
        ],
        out_shape=[
            jax.ShapeDtypeStruct((m, D_MODEL), BF16),
            jax.ShapeDtypeStruct((m, GLA_KEY), F32),
        ],
        compiler_params=_cparams(("parallel",)),
        name="norm_lr",
    )(x, g, w_lr, w_a2, b_a)


def _norm_kernel(x_ref, g_ref, h_ref):
    x = x_ref[...]
    ms = jnp.mean(x * x, axis=-1, keepdims=True)
    h_ref[...] = (x * lax.rsqrt(ms + EPS) * g_ref[...]).astype(h_ref.dtype)


def _norm(x, g):
    m = x.shape[0]
    bm = _pick_block(m, 256)
    return pl.pallas_call(
        _norm_kernel,
        grid=(m // bm,),
        in_specs=[
            pl.BlockSpec((bm, D_MODEL), lambda i: (i, 0)),
            pl.BlockSpec((1, D_MODEL), lambda i: (0, 0)),
        ],
        out_specs=pl.BlockSpec((bm, D_MODEL), lambda i: (i, 0)),
        out_shape=jax.ShapeDtypeStruct((m, D_MODEL), BF16),
        compiler_params=_cparams(("parallel",)),
        name="norm_mlp",
    )(x, g)


def _final_kernel(x_ref, m_ref, g_ref, y_ref):
    x = x_ref[...] + m_ref[...]
    ms = jnp.mean(x * x, axis=-1, keepdims=True)
    y_ref[...] = x * lax.rsqrt(ms + EPS) * g_ref[...]


def _final(x2, mlp, g):
    m = x2.shape[0]
    bm = _pick_block(m, 256)
    return pl.pallas_call(
        _final_kernel,
        grid=(m // bm,),
        in_specs=[
            pl.BlockSpec((bm, D_MODEL), lambda i: (i, 0)),
            pl.BlockSpec((bm, D_MODEL), lambda i: (i, 0)),
            pl.BlockSpec((1, D_MODEL), lambda i: (0, 0)),
        ],
        out_specs=pl.BlockSpec((bm, D_MODEL), lambda i: (i, 0)),
        out_shape=jax.ShapeDtypeStruct((m, D_MODEL), F32),
        compiler_params=_cparams(("parallel",)),
        name="final_norm",
    )(x2, mlp, g)


def _matmul_kernel(a_ref, w_ref, o_ref):
    o_ref[...] = _dot(a_ref[...], w_ref[...])


def _in_proj(h, w):
    m, k = h.shape
    n = w.shape[1]
    bm = _pick_block(m, 1024)
    bn = _pick_block(n, 1024)
    return pl.pallas_call(
        _matmul_kernel,
        grid=(m // bm, n // bn),
        in_specs=[
            pl.BlockSpec((bm, k), lambda i, j: (i, 0)),
            pl.BlockSpec((k, bn), lambda i, j: (0, j)),
        ],
        out_specs=pl.BlockSpec((bm, bn), lambda i, j: (i, j)),
        out_shape=jax.ShapeDtypeStruct((m, n), F32),
        compiler_params=_cparams(("parallel", "arbitrary")),
        name="in_proj",
    )(h, w)


def _merge_kernel(ya_ref, yb_ref, wa_ref, wb_ref, ga_ref, gb_ref, u_ref):
    pa = _dot(ya_ref[...].astype(BF16), wa_ref[...])
    pb = _dot(yb_ref[...].astype(BF16), wb_ref[...])
    u = jax.nn.sigmoid(ga_ref[...]) * pa + jax.nn.sigmoid(gb_ref[...]) * pb
    u_ref[...] = u.astype(BF16)


def _merge(ya, yb, w_pa, w_pb, z):
    m = ya.shape[0]
    bm = _pick_block(m, 512)
    bn = 1024
    ga0, gb0 = Z_GA // bn, Z_GB // bn
    return pl.pallas_call(
        _merge_kernel,
        grid=(m // bm, D_MODEL // bn),
        in_specs=[
            pl.BlockSpec((bm, GLA_WIDTH), lambda i, j: (i, 0)),
            pl.BlockSpec((bm, DIFF_WIDTH), lambda i, j: (i, 0)),
            pl.BlockSpec((GLA_WIDTH, bn), lambda i, j: (0, j)),
            pl.BlockSpec((DIFF_WIDTH, bn), lambda i, j: (0, j)),
            pl.BlockSpec((bm, bn), lambda i, j: (i, ga0 + j)),
            pl.BlockSpec((bm, bn), lambda i, j: (i, gb0 + j)),
        ],
        out_specs=pl.BlockSpec((bm, bn), lambda i, j: (i, j)),
        out_shape=jax.ShapeDtypeStruct((m, D_MODEL), BF16),
        compiler_params=_cparams(("parallel", "arbitrary")),
        name="merge",
    )(ya, yb, w_pa, w_pb, z, z)


def _oproj_kernel(u_ref, w_ref, x_ref, o_ref):
    o_ref[...] = x_ref[...] + _dot(u_ref[...], w_ref[...])


def _oproj(u, w_o, x):
    m = u.shape[0]
    bm = _pick_block(m, 512)
    bn = 1024
    return pl.pallas_call(
        _oproj_kernel,
        grid=(m // bm, D_MODEL // bn),
        in_specs=[
            pl.BlockSpec((bm, D_MODEL), lambda i, j: (i, 0)),
            pl.BlockSpec((D_MODEL, bn), lambda i, j: (0, j)),
            pl.BlockSpec((bm, bn), lambda i, j: (i, j)),
        ],
        out_specs=pl.BlockSpec((bm, bn), lambda i, j: (i, j)),
        out_shape=jax.ShapeDtypeStruct((m, D_MODEL), F32),
        compiler_params=_cparams(("parallel", "arbitrary")),
        name="o_proj",
    )(u, w_o, x)


def _mlp_kernel(h_ref, wu_ref, wd_ref, o_ref):
    f = pl.program_id(1)
    t = jnp.square(jnp.maximum(_dot(h_ref[...], wu_ref[...]), 0.0)).astype(BF16)
    part = _dot(t, wd_ref[...])

    @pl.when(f == 0)
    def _():
        o_ref[...] = part

    @pl.when(f > 0)
    def _():
        o_ref[...] += part


def _mlp(h2, w_up, w_down):
    m = h2.shape[0]
    d_ff = w_up.shape[1]
    bm = _pick_block(m, 512)
    bf = _pick_block(d_ff, 512)
    return pl.pallas_call(
        _mlp_kernel,
        grid=(m // bm, d_ff // bf),
        in_specs=[
            pl.BlockSpec((bm, D_MODEL), lambda i, f: (i, 0)),
            pl.BlockSpec((D_MODEL, bf), lambda i, f: (0, f)),
            pl.BlockSpec((bf, D_MODEL), lambda i, f: (f, 0)),
        ],
        out_specs=pl.BlockSpec((bm, D_MODEL), lambda i, f: (i, 0)),
        out_shape=jax.ShapeDtypeStruct((m, D_MODEL), F32),
        compiler_params=_cparams(("parallel", "arbitrary")),
        name="mlp",
    )(h2, w_up, w_down)


def _split3(x):
    hi = x.astype(BF16)
    r1 = x - hi.astype(F32)
    mid = r1.astype(BF16)
    lo = (r1 - mid.astype(F32)).astype(BF16)
    return hi, mid, lo


def _select_sum(sel, x):
    hi, mid, lo = _split3(x)
    return _dot(sel, hi) + _dot(sel, mid) + _dot(sel, lo)


def _group_row_bcast(x, group, row):
    r, c = x.shape
    x3 = x.reshape(r // group, group, c)
    return jnp.broadcast_to(x3[:, row:row + 1, :], x3.shape).reshape(r, c)


def _gla_intra(q, k, b, sub, span):
    r, dk = q.shape
    n = r // sub
    q3 = q.reshape(n, sub, dk)
    k3 = k.reshape(n, sub, dk)
    b3 = b.reshape(n, sub, dk)
    rowi = lax.broadcasted_iota(jnp.int32, (n, sub, dk), 1)
    lane = lax.broadcasted_iota(jnp.int32, (n, sub, dk), 2)
    band0 = lax.broadcasted_iota(jnp.int32, (n, sub, dk), 0) * sub
    a3 = jnp.zeros((n, sub, dk), F32)
    for s in range(sub):
        bs = b3[:, s:s + 1, :]
        ks = k3[:, s:s + 1, :]
        e = jnp.exp(jnp.minimum(b3 - bs, 0.0))
        val = jnp.sum(q3 * e * ks, axis=-1, keepdims=True)
        a3 = jnp.where((lane == band0 + s) & (rowi >= s), val, a3)
    a = a3.reshape(r, dk)

    row = lax.broadcasted_iota(jnp.int32, (r, r), 0)
    col = lax.broadcasted_iota(jnp.int32, (r, r), 1)
    row1 = lax.broadcasted_iota(jnp.int32, (r, 1), 0)
    half = sub
    while half < span:
        group = 2 * half
        ref_b = _group_row_bcast(b, group, half - 1)
        upper = (row1 % group) >= half
        qh = jnp.where(upper, q * jnp.exp(jnp.minimum(b - ref_b, 0.0)), 0.0)
        kh = jnp.where(upper, 0.0, k * jnp.exp(jnp.minimum(ref_b - b, 0.0)))
        a_l = _dot_nt(qh.astype(BF16), kh.astype(BF16))
        a = a + jnp.where((row // group) == (col // group), a_l, 0.0)
        half = group
    return a


def _gla_head_out(o, gn, r):
    o = o * lax.rsqrt(jnp.mean(o * o, axis=-1, keepdims=True) + EPS) * gn
    return o * (r * jax.nn.sigmoid(r))


def _gla_prompt_kernel(q_ref, k_ref, v_ref, r_ref, la_ref, gn_ref, y_ref, sfin_ref, s_ref):
    i = pl.program_id(1)
    rows = q_ref.shape[0]

    @pl.when(i == 0)
    def _():
        s_ref[...] = jnp.zeros_like(s_ref)

    q = q_ref[...] * (GLA_DK ** -0.5)
    k = k_ref[...]
    v = v_ref[...].astype(BF16)
    la = la_ref[...]

    row = lax.broadcasted_iota(jnp.int32, (rows, rows), 0)
    col = lax.broadcasted_iota(jnp.int32, (rows, rows), 1)
    tri = (col <= row).astype(BF16)
    b = _select_sum(tri, la)
    b_last = b[rows - 1:rows, :]

    a = _gla_intra(q, k, b, GLA_SUB, rows)
    s_old = s_ref[...]
    o = _dot((q * jnp.exp(b)).astype(BF16), s_old.astype(BF16)) + _dot(a.astype(BF16), v)
    y_ref[...] = _gla_head_out(o, gn_ref[...], r_ref[...])

    kd_t = (k * jnp.exp(b_last - b)).T
    decay = jnp.exp(jnp.sum(la.T, axis=1, keepdims=True))
    s_new = decay * s_old + _dot(kd_t.astype(BF16), v)
    s_ref[...] = s_new

    @pl.when(i == pl.num_programs(1) - 1)
    def _():
        sfin_ref[0] = s_new


def _gla_prompt(z, la, gn):
    t = z.shape[0]
    rows = GLA_ROWS
    qk = GLA_DK
    vv = GLA_DV
    return pl.pallas_call(
        _gla_prompt_kernel,
        grid=(GLA_HEADS, t // rows),
        in_specs=[
            pl.BlockSpec((rows, qk), lambda h, i: (i, Z_GQ // qk + h)),
            pl.BlockSpec((rows, qk), lambda h, i: (i, Z_GK // qk + h)),
            pl.BlockSpec((rows, vv), lambda h, i: (i, Z_GV // vv + h)),
            pl.BlockSpec((rows, vv), lambda h, i: (i, Z_GR // vv + h)),
            pl.BlockSpec((rows, qk), lambda h, i: (i, h)),
            pl.BlockSpec((1, vv), lambda h, i: (0, h)),
        ],
        out_specs=[
            pl.BlockSpec((rows, vv), lambda h, i: (i, h)),
            pl.BlockSpec((1, qk, vv), lambda h, i: (h, 0, 0)),
        ],
        out_shape=[
            jax.ShapeDtypeStruct((t, GLA_WIDTH), F32),
            jax.ShapeDtypeStruct((GLA_HEADS, qk, vv), F32),
        ],
        scratch_shapes=[pltpu.VMEM((qk, vv), F32)],
        compiler_params=_cparams(("parallel", "arbitrary")),
        name="gla_prompt",
    )(z, z, z, z, la, gn)


def _gla_sample_kernel(q_ref, k_ref, v_ref, r_ref, la_ref, gn_ref, s0_ref, y_ref, sfin_ref):
    rows = q_ref.shape[0]
    n_seq = rows // SAMPLE_T
    q = q_ref[...] * (GLA_DK ** -0.5)
    k = k_ref[...]
    v = v_ref[...].astype(BF16)
    la = la_ref[...]

    row = lax.broadcasted_iota(jnp.int32, (rows, rows), 0)
    col = lax.broadcasted_iota(jnp.int32, (rows, rows), 1)
    same_seq = (row // SAMPLE_T) == (col // SAMPLE_T)
    tri = ((col <= row) & same_seq).astype(BF16)
    b = _select_sum(tri, la)
    b_last = _group_row_bcast(b, SAMPLE_T, SAMPLE_T - 1)

    a = _gla_intra(q, k, b, SAMPLE_T, SAMPLE_T)
    o = _dot(a.astype(BF16), v)

    qd = q * jnp.exp(b)
    kd_t = (k * jnp.exp(b_last - b)).T
    la_t = la.T
    row1 = lax.broadcasted_iota(jnp.int32, (rows, 1), 0)
    lane1 = lax.broadcasted_iota(jnp.int32, (1, rows), 1)
    for sq in range(n_seq):
        s0 = s0_ref[sq, 0]
        in_rows = (row1 // SAMPLE_T) == sq
        in_lanes = (lane1 // SAMPLE_T) == sq
        o = o + _dot(jnp.where(in_rows, qd, 0.0).astype(BF16), s0.astype(BF16))
        decay = jnp.exp(jnp.sum(jnp.where(in_lanes, la_t, 0.0), axis=1, keepdims=True))
        sfin_ref[sq, 0] = decay * s0 + _dot(jnp.where(in_lanes, kd_t, 0.0).astype(BF16), v)
    y_ref[...] = _gla_head_out(o, gn_ref[...], r_ref[...])


def _gla_sample(z, la, gn, state):
    m = z.shape[0]
    rows = GLA_ROWS
    n_seq = rows // SAMPLE_T
    qk = GLA_DK
    vv = GLA_DV
    return pl.pallas_call(
        _gla_sample_kernel,
        grid=(GLA_HEADS, m // rows),
        in_specs=[
            pl.BlockSpec((rows, qk), lambda h, i: (i, Z_GQ // qk + h)),
            pl.BlockSpec((rows, qk), lambda h, i: (i, Z_GK // qk + h)),
            pl.BlockSpec((rows, vv), lambda h, i: (i, Z_GV // vv + h)),
            pl.BlockSpec((rows, vv), lambda h, i: (i, Z_GR // vv + h)),
            pl.BlockSpec((rows, qk), lambda h, i: (i, h)),
            pl.BlockSpec((1, vv), lambda h, i: (0, h)),
            pl.BlockSpec((n_seq, 1, qk, vv), lambda h, i: (i, h, 0, 0)),
        ],
        out_specs=[
            pl.BlockSpec((rows, vv), lambda h, i: (i, h)),
            pl.BlockSpec((n_seq, 1, qk, vv), lambda h, i: (i, h, 0, 0)),
        ],
        out_shape=[
            jax.ShapeDtypeStruct((m, GLA_WIDTH), F32),
            jax.ShapeDtypeStruct(state.shape, F32),
        ],
        compiler_params=_cparams(("parallel", "parallel")),
        name="gla_sample",
    )(z, z, z, z, la, gn, state)


def _lambda_value(lam_ref, lam_init):
    l = lam_ref[...]
    s1 = jnp.sum(l[0:1] * l[1:2], axis=-1, keepdims=True)
    s2 = jnp.sum(l[2:3] * l[3:4], axis=-1, keepdims=True)
    return jnp.exp(s1) - jnp.exp(s2) + lam_init


def _softmax_step(s, v, m_ref, l_ref, acc_ref, idx):
    m_old = m_ref[idx]
    m_new = jnp.maximum(m_old, jnp.max(s, axis=-1, keepdims=True))
    alpha = jnp.exp(m_old - m_new)
    p = jnp.exp(s - m_new)
    l_ref[idx] = alpha * l_ref[idx] + jnp.sum(p, axis=-1, keepdims=True)
    acc_ref[idx] = alpha * acc_ref[idx] + _dot(p.astype(BF16), v)
    m_ref[idx] = m_new


def _diff_head_out(acc1, l1, acc2, l2, lam, gn, lam_init):
    o = acc1 / l1 - lam * (acc2 / l2)
    o = o * lax.rsqrt(jnp.mean(o * o, axis=-1, keepdims=True) + EPS)
    return o * gn * (1.0 - lam_init)


def _diff_prompt_kernel(lam_ref, q_ref, k_ref, v_ref, gn_ref, y_ref,
                        q_s, m_s, l_s, acc_s, *, lam_init):
    h = pl.program_id(0)
    qi = pl.program_id(1)
    ki = pl.program_id(2)
    bq = q_ref.shape[0]
    bk = k_ref.shape[0]

    @pl.when(ki == 0)
    def _():
        q_s[...] = q_ref[...].astype(BF16)
        m_s[...] = jnp.full_like(m_s, NEG_BIG)
        l_s[...] = jnp.zeros_like(l_s)
        acc_s[...] = jnp.zeros_like(acc_s)

    def step(masked):
        k = k_ref[...].astype(BF16)
        v = v_ref[...].astype(BF16)
        head = jnp.full((1, bk), h + 1, jnp.int32).astype(F32)
        slope = jnp.exp2(-head * (8.0 / DIFF_HEADS))
        kpos =(ki * bk - qi * bq + lax.broadcasted_iota(jnp.int32, (1, bk), 1)).astype(F32)
        bias = slope * kpos
        if masked:
            row = lax.broadcasted_iota(jnp.int32, (bq, bk), 0)
            col = lax.broadcasted_iota(jnp.int32, (bq, bk), 1)
            keep = col <= row
        for mp in range(2):
            c = mp * DIFF_DK
            s = _dot_nt(q_s[:, c:c + DIFF_DK], k[:, c:c + DIFF_DK]) * (DIFF_DK ** -0.5) + bias
            if masked:
                s = jnp.where(keep, s, -jnp.inf)
            _softmax_step(s, v, m_s, l_s, acc_s, mp)

    @pl.when(ki < qi)
    def _():
        step(False)

    @pl.when(ki == qi)
    def _():
        step(True)
        lam = _lambda_value(lam_ref, lam_init)
        y_ref[...] = _diff_head_out(acc_s[0], l_s[0], acc_s[1], l_s[1], lam, gn_ref[...], lam_init)


def _diff_prompt(z, lam_vecs, gn, lam_init):
    t = z.shape[0]
    bq = _pick_block(t, 512)
    w = 2 * DIFF_DK
    nq = t // bq
    return pl.pallas_call(
        functools.partial(_diff_prompt_kernel, lam_init=lam_init),
        grid=(DIFF_HEADS, nq, nq),
        in_specs=[
            pl.BlockSpec((4, DIFF_DK), lambda h, qi, ki: (0, 0)),
            pl.BlockSpec((bq, w), lambda h, qi, ki: (qi, Z_DQ // w + h)),
            pl.BlockSpec((bq, w), lambda h, qi, ki: (jnp.minimum(ki, qi), Z_DK // w + h)),
            pl.BlockSpec((bq, DIFF_DV), lambda h, qi, ki: (jnp.minimum(ki, qi), Z_DV // DIFF_DV + h)),
            pl.BlockSpec((1, DIFF_DV), lambda h, qi, ki: (0, h)),
        ],
        out_specs=pl.BlockSpec((bq, DIFF_DV), lambda h, qi, ki: (qi, h)),
        out_shape=jax.ShapeDtypeStruct((t, DIFF_WIDTH), F32),
        scratch_shapes=[
            pltpu.VMEM((bq, w), BF16),
            pltpu.VMEM((2, bq, 1), F32),
            pltpu.VMEM((2, bq, 1), F32),
            pltpu.VMEM((2, bq, DIFF_DV), F32),
        ],
        compiler_params=_cparams(("parallel", "parallel", "arbitrary")),
        name="diff_prompt",
    )(lam_vecs, z, z, z, gn)


def _diff_sample_kernel(pt_ref, lam_ref, q_ref, kn_ref, vn_ref, kp_ref, vp_ref, gn_ref, y_ref,
                        q_s, kn_s, vn_s, m_s, l_s, acc_s, *, lam_init, past_len):
    del pt_ref
    p = pl.program_id(1)
    tn = q_ref.shape[0]
    page = kp_ref.shape[1]
    w = 2 * DIFF_DK
    n_rows = DIFF_HEADS * 2 * tn

    rowh = lax.broadcasted_iota(jnp.int32, (n_rows, 1), 0) // (2 * tn)
    slope = jnp.exp2(-(rowh + 1).astype(F32) * (8.0 / DIFF_HEADS))

    def attend(kb, vb, kpos_rel, keep):
        parts = []
        for hh in range(DIFF_HEADS):
            for mp in range(2):
                c = hh * w + mp * DIFF_DK
                parts.append(_dot_nt(q_s[:, c:c + DIFF_DK], kb[:, c:c + DIFF_DK]))
        s = jnp.concatenate(parts, axis=0) * (DIFF_DK ** -0.5) + slope * kpos_rel
        if keep is not None:
            s = jnp.where(keep, s, -jnp.inf)
        m_old = m_s[...]
        m_new = jnp.maximum(m_old, jnp.max(s, axis=-1, keepdims=True))
        alpha = jnp.exp(m_old - m_new)
        pr = jnp.exp(s - m_new)
        l_s[...] = alpha * l_s[...] + jnp.sum(pr, axis=-1, keepdims=True)
        m_s[...] = m_new
        prb = pr.astype(BF16)
        pv = [_dot(prb[hh * 2 * tn:(hh + 1) * 2 * tn, :], vb[:, hh * DIFF_DV:(hh + 1) * DIFF_DV])
              for hh in range(DIFF_HEADS)]
        acc_s[...] = alpha * acc_s[...] + jnp.concatenate(pv, axis=0)

    @pl.when(p == 0)
    def _():
        q_s[...] = q_ref[...].astype(BF16)
        m_s[...] = jnp.full_like(m_s, NEG_BIG)
        l_s[...] = jnp.zeros_like(l_s)
        acc_s[...] = jnp.zeros_like(acc_s)
        kn_s[...] = jnp.zeros_like(kn_s)
        vn_s[...] = jnp.zeros_like(vn_s)
        kn_s[0:tn, :] = kn_ref[...]
        vn_s[0:tn, :] = vn_ref[...]
        colj = lax.broadcasted_iota(jnp.int32, (n_rows, page), 1)
        rowt = lax.broadcasted_iota(jnp.int32, (n_rows, page), 0) % tn
        kpos = lax.broadcasted_iota(jnp.int32, (1, page), 1).astype(F32)
        attend(kn_s[...].astype(BF16), vn_s[...].astype(BF16), kpos, colj <= rowt)

    kpos = (p * page - past_len + lax.broadcasted_iota(jnp.int32, (1, page), 1)).astype(F32)
    attend(kp_ref[0].astype(BF16), vp_ref[0].astype(BF16), kpos, None)

    @pl.when(p == pl.num_programs(1) - 1)
    def _():
        lam = _lambda_value(lam_ref, lam_init)
        acc = acc_s[...]
        l = l_s[...]
        for hh in range(DIFF_HEADS):
            r0 = hh * 2 * tn
            y_ref[:, hh * DIFF_DV:(hh + 1) * DIFF_DV] = _diff_head_out(
                acc[r0:r0 + tn], l[r0:r0 + tn], acc[r0 + tn:r0 + 2 * tn], l[r0 + tn:r0 + 2 * tn],
                lam, gn_ref[:, hh * DIFF_DV:(hh + 1) * DIFF_DV], lam_init)


def _diff_sample(z, cache_k, cache_v, page_table, lam_vecs, gn, lam_init):
    m = z.shape[0]
    tn = SAMPLE_T
    n_batch = m // tn
    n_pages = page_table.shape[1]
    n_pool, page = cache_k.shape[0], cache_k.shape[1]
    w_all = DIFF_QK
    ck = cache_k.reshape(n_pool, page, w_all)
    cv = cache_v.reshape(n_pool, page, DIFF_WIDTH)
    pt = page_table.reshape(-1)
    n_rows = DIFF_HEADS * 2 * tn
    grid_spec = pltpu.PrefetchScalarGridSpec(
        num_scalar_prefetch=1,
        grid=(n_batch, n_pages),
        in_specs=[
            pl.BlockSpec((4, DIFF_DK), lambda b, p, pt: (0, 0)),
            pl.BlockSpec((tn, w_all), lambda b, p, pt: (b, Z_DQ // w_all)),
            pl.BlockSpec((tn, w_all), lambda b, p, pt: (b, Z_DK // w_all)),
            pl.BlockSpec((tn, DIFF_WIDTH), lambda b, p, pt: (b, Z_DV // DIFF_WIDTH)),
            pl.BlockSpec((1, page, w_all), lambda b, p, pt: (pt[b * n_pages + p], 0, 0)),
            pl.BlockSpec((1, page, DIFF_WIDTH), lambda b, p, pt: (pt[b * n_pages + p], 0, 0)),
            pl.BlockSpec((1, DIFF_WIDTH), lambda b, p, pt: (0, 0)),
        ],
        out_specs=pl.BlockSpec((tn, DIFF_WIDTH), lambda b, p, pt: (b, 0)),
        scratch_shapes=[
            pltpu.VMEM((tn, w_all), BF16),
            pltpu.VMEM((page, w_all), F32),
            pltpu.VMEM((page, DIFF_WIDTH), F32),
            pltpu.VMEM((n_rows, 1), F32),
            pltpu.VMEM((n_rows, 1), F32),
            pltpu.VMEM((n_rows, DIFF_DV), F32),
        ],
    )
    return pl.pallas_call(
        functools.partial(_diff_sample_kernel, lam_init=lam_init, past_len=n_pages * page),
        grid_spec=grid_spec,
        out_shape=jax.ShapeDtypeStruct((m, DIFF_WIDTH), F32),
        compiler_params=_cparams(("parallel", "arbitrary")),
        name="diff_sample",
    )(pt, lam_vecs, z, z, z, ck, cv, gn)


def _prep_weights(w_in, w_alpha2, b_alpha, w_proj_a, w_proj_b, w_o, w_up, w_down):
    w_main = jnp.concatenate(
        [w_in[:, :LR_START], w_in[:, LR_START + GLA_RANK:]], axis=1).astype(BF16)
    w_lr = jnp.pad(w_in[:, LR_START:LR_START + GLA_RANK],
                   ((0, 0), (0, V7X_LANES - GLA_RANK))).astype(BF16)
    w_a2 = jnp.pad(w_alpha2, ((0, V7X_LANES - GLA_RANK), (0, 0))).astype(BF16)
    return dict(w_main=w_main, w_lr=w_lr, w_a2=w_a2, b_a=b_alpha.reshape(1, GLA_KEY),
                w_pa=w_proj_a.astype(BF16), w_pb=w_proj_b.astype(BF16), w_o=w_o.astype(BF16),
                w_up=w_up.astype(BF16), w_down=w_down.astype(BF16))


def _mix_inputs(x, norm_g, wts):
    h, la = _norm_lr(x, norm_g.reshape(1, D_MODEL), wts["w_lr"], wts["w_a2"], wts["b_a"])
    z = _in_proj(h, wts["w_main"])
    return z, la


def _channel_mix(x, ya, yb, z, wts, norm_mlp):
    u = _merge(ya, yb, wts["w_pa"], wts["w_pb"], z)
    x2 = _oproj(u, wts["w_o"], x)
    h2 = _norm(x2, norm_mlp.reshape(1, D_MODEL))
    return x2, _mlp(h2, wts["w_up"], wts["w_down"])


def kernel(x_prompt, x_sample, cache_k, cache_v, state_gla, page_table, norm_mix, w_in, w_alpha2,
           b_alpha, gla_norm, lambda_q1, lambda_k1, lambda_q2, lambda_k2, diff_norm, w_proj_a,
           w_proj_b, w_o, norm_mlp, w_up, w_down, final_norm):
    depth = w_in.shape[0]
    bp, tp = x_prompt.shape[0], x_prompt.shape[1]
    bs, ts = x_sample.shape[0], x_sample.shape[1]
    assert bp == 1 and ts == SAMPLE_T
    xp = x_prompt.reshape(tp, D_MODEL)
    xs = x_sample.reshape(bs * ts, D_MODEL)
    fin = final_norm.reshape(1, D_MODEL)
    kp_l, vp_l, sp_l, ks_l, vs_l, ss_l = [], [], [], [], [], []
    for l in range(depth):
        lam_init = 0.8 - 0.6 * math.exp(-0.3 * l)
        wts = _prep_weights(w_in[l], w_alpha2[l], b_alpha[l], w_proj_a[l], w_proj_b[l], w_o[l],
                            w_up[l], w_down[l])
        lam_vecs = jnp.stack([lambda_q1[l], lambda_k1[l], lambda_q2[l], lambda_k2[l]]).astype(F32)
        gn_a = gla_norm[l].reshape(1, GLA_WIDTH)
        gn_b = diff_norm[l].reshape(1, DIFF_WIDTH)
        last = l == depth - 1

        z, la = _mix_inputs(xp, norm_mix[l], wts)
        ya, s_fin = _gla_prompt(z, la, gn_a)
        yb = _diff_prompt(z, lam_vecs, gn_b, lam_init)
        x2, mlp = _channel_mix(xp, ya, yb, z, wts, norm_mlp[l])
        xp = _final(x2, mlp, fin) if last else x2 + mlp
        kp_l.append(z[:, Z_DK:Z_DK + DIFF_QK].reshape(bp, tp, DIFF_HEADS, 2 * DIFF_DK))
        vp_l.append(z[:, Z_DV:Z_DV + DIFF_WIDTH].reshape(bp, tp, DIFF_HEADS, DIFF_DV))
        sp_l.append(s_fin.reshape(bp, GLA_HEADS, GLA_DK, GLA_DV))

        z, la = _mix_inputs(xs, norm_mix[l], wts)
        ya, s_fin = _gla_sample(z, la, gn_a, state_gla[l])
        yb = _diff_sample(z, cache_k[l], cache_v[l], page_table, lam_vecs, gn_b, lam_init)
        x2, mlp = _channel_mix(xs, ya, yb, z, wts, norm_mlp[l])
        xs = _final(x2, mlp, fin) if last else x2 + mlp
        ks_l.append(z[:, Z_DK:Z_DK + DIFF_QK].reshape(bs, ts, DIFF_HEADS, 2 * DIFF_DK))
        vs_l.append(z[:, Z_DV:Z_DV + DIFF_WIDTH].reshape(bs, ts, DIFF_HEADS, DIFF_DV))
        ss_l.append(s_fin)

    return (xp.reshape(bp, tp, D_MODEL), xs.reshape(bs, ts, D_MODEL), jnp.stack(kp_l),
            jnp.stack(vp_l), jnp.stack(sp_l), jnp.stack(ks_l), jnp.stack(vs_l), jnp.stack(ss_l))
```

```python
import functools
import math

import jax
import jax.numpy as jnp
from jax import lax
from jax.experimental import pallas as pl
from jax.experimental.pallas import tpu as pltpu

F32 = jnp.float32
BF16 = jnp.bfloat16

D_MODEL = 4096
PAGE_SIZE = 128
GLA_HEADS = 8
GLA_DK = 128
GLA_DV = 256
GLA_KEY = GLA_HEADS * GLA_DK
GLA_WIDTH = GLA_HEADS * GLA_DV
GLA_RANK = 16
GLA_TAU = 16.0
DIFF_HEADS = 8
DIFF_DK = 128
DIFF_DV = 256
DIFF_QK = DIFF_HEADS * 2 * DIFF_DK
DIFF_WIDTH = DIFF_HEADS * DIFF_DV
EPS = 1e-6

Z_GQ = 0
Z_GK = Z_GQ + GLA_KEY
Z_GV = Z_GK + GLA_KEY
Z_GR = Z_GV + GLA_WIDTH
Z_DQ = Z_GR + GLA_WIDTH
Z_DK = Z_DQ + DIFF_QK
Z_DV = Z_DK + DIFF_QK
Z_GA = Z_DV + DIFF_WIDTH
Z_GB = Z_GA + D_MODEL
Z_COLS = Z_GB + D_MODEL
LR_START = 2 * GLA_KEY + 2 * GLA_WIDTH

V7X_LANES = 128
V7X_SUBLANES = 8
V7X_VMEM_LIMIT_BYTES = 56 * 1024 * 1024

GLA_ROWS = 128
GLA_SUB = 16
SAMPLE_T = 8
NEG_BIG = -1e30
LOG2_E = 1.4426950408889634


def _pick_block(n, pref, mult=V7X_LANES):
    b = (min(pref, n) // mult) * mult
    while b > mult and n % b:
        b -= mult
    assert b > 0 and n % b == 0, (n, pref, mult)
    return b


def _cparams(sem):
    return pltpu.CompilerParams(dimension_semantics=sem, vmem_limit_bytes=V7X_VMEM_LIMIT_BYTES)


def _dot(a, b):
    return jnp.dot(a, b, preferred_element_type=F32)


def _dot_nt(a, b):
    return lax.dot_general(a, b, (((1,), (1,)), ((), ())), preferred_element_type=F32)


def _norm_lr_kernel(x_ref, g_ref, wlr_ref, wa2_ref, ba_ref, h_ref, la_ref):
    x = x_ref[...]
    ms = jnp.mean(x * x, axis=-1, keepdims=True)
    h = (x * lax.rsqrt(ms + EPS) * g_ref[...]).astype(BF16)
    h_ref[...] = h
    g_lr = _dot(h, wlr_ref[...])
    xa = _dot(g_lr.astype(BF16), wa2_ref[...]) + ba_ref[...]
    la_ref[...] = jax.nn.log_sigmoid(xa) * (1.0 / GLA_TAU)


def _norm_lr(x, g, w_lr, w_a2, b_a):
    m = x.shape[0]
    bm = _pick_block(m, 256)
    return pl.pallas_call(
        _norm_lr_kernel,
        grid=(m // bm,),
        in_specs=[
            pl.BlockSpec((bm, D_MODEL), lambda i: (i, 0)),
            pl.BlockSpec((1, D_MODEL), lambda i: (0, 0)),
            pl.BlockSpec((D_MODEL, V7X_LANES), lambda i: (0, 0)),
            pl.BlockSpec((V7X_LANES, GLA_KEY), lambda i: (0, 0)),
            pl.BlockSpec((1, GLA_KEY), lambda i: (0, 0)),
        ],
        out_specs=[
            pl.BlockSpec((bm, D_MODEL), lambda i: (i, 0)),
            pl.BlockSpec((bm, GLA_KEY), lambda i: (i, 0)),
        ],
        out_shape=[
            jax.ShapeDtypeStruct((m, D_MODEL), BF16),
            jax.ShapeDtypeStruct((m, GLA_KEY), F32),
        ],
        compiler_params=_cparams(("parallel",)),
        name="norm_lr",
    )(x, g, w_lr, w_a2, b_a)


def _norm_kernel(x_ref, g_ref, h_ref):
    x = x_ref[...]
    ms = jnp.mean(x * x, axis=-1, keepdims=True)
    h_ref[...] = (x * lax.rsqrt(ms + EPS) * g_ref[...]).astype(h_ref.dtype)


def _norm(x, g):
    m = x.shape[0]
    bm = _pick_block(m, 256)
    return pl.pallas_call(
        _norm_kernel,
        grid=(m // bm,),
        in_specs=[
            pl.BlockSpec((bm, D_MODEL), lambda i: (i, 0)),
            pl.BlockSpec((1, D_MODEL), lambda i: (0, 0)),
        ],
        out_specs=pl.BlockSpec((bm, D_MODEL), lambda i: (i, 0)),
        out_shape=jax.ShapeDtypeStruct((m, D_MODEL), BF16),
        compiler_params=_cparams(("parallel",)),
        name="norm_mlp",
    )(x, g)


def _final_kernel(x_ref, m_ref, g_ref, y_ref):
    x = x_ref[...] + m_ref[...]
    ms = jnp.mean(x * x, axis=-1, keepdims=True)
    y_ref[...] = x * lax.rsqrt(ms + EPS) * g_ref[...]


def _final(x2, mlp, g):
    m = x2.shape[0]
    bm = _pick_block(m, 256)
    return pl.pallas_call(
        _final_kernel,
        grid=(m // bm,),
        in_specs=[
            pl.BlockSpec((bm, D_MODEL), lambda i: (i, 0)),
            pl.BlockSpec((bm, D_MODEL), lambda i: (i, 0)),
            pl.BlockSpec((1, D_MODEL), lambda i: (0, 0)),
        ],
        out_specs=pl.BlockSpec((bm, D_MODEL), lambda i: (i, 0)),
        out_shape=jax.ShapeDtypeStruct((m, D_MODEL), F32),
        compiler_params=_cparams(("parallel",)),
        name="final_norm",
    )(x2, mlp, g)


def _matmul_kernel(a_ref, w_ref, o_ref):
    o_ref[...] = _dot(a_ref[...], w_ref[...])


def _in_proj(h, w):
    m, k = h.shape
    n = w.shape[1]
    bm = _pick_block(m, 1024)
    bn = _pick_block(n, 1024)
    return pl.pallas_call(
        _matmul_kernel,
        grid=(m // bm, n // bn),
        in_specs=[
            pl.BlockSpec((bm, k), lambda i, j: (i, 0)),
            pl.BlockSpec((k, bn), lambda i, j: (0, j)),
        ],
        out_specs=pl.BlockSpec((bm, bn), lambda i, j: (i, j)),
        out_shape=jax.ShapeDtypeStruct((m, n), F32),
        compiler_params=_cparams(("parallel", "arbitrary")),
        name="in_proj",
    )(h, w)


def _vt_proj_kernel(w_ref, h_ref, o_ref):
    o_ref[...] = _dot_nt(w_ref[...], h_ref[...]).astype(BF16)


def _vt_proj(w_t, h):
    n, k = w_t.shape
    m = h.shape[0]
    bm = _pick_block(m, 1024)
    bn = _pick_block(n, 1024)
    return pl.pallas_call(
        _vt_proj_kernel,
        grid=(m // bm, n // bn),
        in_specs=[
            pl.BlockSpec((bn, k), lambda i, j: (j, 0)),
            pl.BlockSpec((bm, k), lambda i, j: (i, 0)),
        ],
        out_specs=pl.BlockSpec((bn, bm), lambda i, j: (j, i)),
        out_shape=jax.ShapeDtypeStruct((n, m), BF16),
        compiler_params=_cparams(("parallel", "arbitrary")),
        name="vt_proj",
    )(w_t, h)


def _merge_kernel(ya_ref, yb_ref, wa_ref, wb_ref, ga_ref, gb_ref, u_ref):
    pa = _dot(ya_ref[...].astype(BF16), wa_ref[...])
    pb = _dot(yb_ref[...].astype(BF16), wb_ref[...])
    u = jax.nn.sigmoid(ga_ref[...]) * pa + jax.nn.sigmoid(gb_ref[...]) * pb
    u_ref[...] = u.astype(BF16)


def _merge(ya, yb, w_pa, w_pb, z):
    m = ya.shape[0]
    bm = _pick_block(m, 512)
    bn = 1024
    ga0, gb0 = Z_GA // bn, Z_GB // bn
    return pl.pallas_call(
        _merge_kernel,
        grid=(m // bm, D_MODEL // bn),
        in_specs=[
            pl.BlockSpec((bm, GLA_WIDTH), lambda i, j: (i, 0)),
            pl.BlockSpec((bm, DIFF_WIDTH), lambda i, j: (i, 0)),
            pl.BlockSpec((GLA_WIDTH, bn), lambda i, j: (0, j)),
            pl.BlockSpec((DIFF_WIDTH, bn), lambda i, j: (0, j)),
            pl.BlockSpec((bm, bn), lambda i, j: (i, ga0 + j)),
            pl.BlockSpec((bm, bn), lambda i, j: (i, gb0 + j)),
        ],
        out_specs=pl.BlockSpec((bm, bn), lambda i, j: (i, j)),
        out_shape=jax.ShapeDtypeStruct((m, D_MODEL), BF16),
        compiler_params=_cparams(("parallel", "arbitrary")),
        name="merge",
    )(ya, yb, w_pa, w_pb, z, z)


def _oproj_kernel(u_ref, w_ref, x_ref, o_ref):
    o_ref[...] = x_ref[...] + _dot(u_ref[...], w_ref[...])


def _oproj(u, w_o, x):
    m = u.shape[0]
    bm = _pick_block(m, 1024)
    bn = 1024
    return pl.pallas_call(
        _oproj_kernel,
        grid=(m // bm, D_MODEL // bn),
        in_specs=[
            pl.BlockSpec((bm, D_MODEL), lambda i, j: (i, 0)),
            pl.BlockSpec((D_MODEL, bn), lambda i, j: (0, j)),
            pl.BlockSpec((bm, bn), lambda i, j: (i, j)),
        ],
        out_specs=pl.BlockSpec((bm, bn), lambda i, j: (i, j)),
        out_shape=jax.ShapeDtypeStruct((m, D_MODEL), F32),
        compiler_params=_cparams(("parallel", "arbitrary")),
        name="o_proj",
    )(u, w_o, x)


def _mlp_kernel(h_ref, wu_ref, wd_ref, o_ref, t0_s, t1_s):
    f = pl.program_id(1)
    last = pl.num_programs(1) - 1

    def up():
        return jnp.square(jnp.maximum(_dot(h_ref[...], wu_ref[...]), 0.0)).astype(BF16)

    @pl.when(f == 0)
    def _():
        t0_s[...] = up()

    @pl.when(f == 1)
    def _():
        o_ref[...] = _dot(t0_s[...], wd_ref[...])
        t1_s[...] = up()

    def middle(t_prev, t_next):
        o_ref[...] += _dot(t_prev[...], wd_ref[...])
        t_next[...] = up()

    @pl.when((f > 1) & (f < last) & (f % 2 == 0))
    def _():
        middle(t1_s, t0_s)

    @pl.when((f > 1) & (f < last) & (f % 2 == 1))
    def _():
        middle(t0_s, t1_s)

    @pl.when((f == last) & (f % 2 == 0))
    def _():
        o_ref[...] += _dot(t1_s[...], wd_ref[...])

    @pl.when((f == last) & (f % 2 == 1))
    def _():
        o_ref[...] += _dot(t0_s[...], wd_ref[...])


def _mlp(h2, w_up, w_down):
    m = h2.shape[0]
    d_ff = w_up.shape[1]
    bm = _pick_block(m, 512)
    bf = _pick_block(d_ff, 512)
    nf = d_ff // bf
    assert nf >= 2
    return pl.pallas_call(
        _mlp_kernel,
        grid=(m // bm, nf + 1),
        in_specs=[
            pl.BlockSpec((bm, D_MODEL), lambda i, f: (i, 0)),
            pl.BlockSpec((D_MODEL, bf), lambda i, f: (0, jnp.minimum(f, nf - 1))),
            pl.BlockSpec((bf, D_MODEL), lambda i, f: (jnp.maximum(f - 1, 0), 0)),
        ],
        out_specs=pl.BlockSpec((bm, D_MODEL), lambda i, f: (i, 0)),
        out_shape=jax.ShapeDtypeStruct((m, D_MODEL), F32),
        scratch_shapes=[pltpu.VMEM((bm, bf), BF16), pltpu.VMEM((bm, bf), BF16)],
        compiler_params=_cparams(("parallel", "arbitrary")),
        name="mlp",
    )(h2, w_up, w_down)


def _split3(x):
    hi = x.astype(BF16)
    r1 = x - hi.astype(F32)
    mid = r1.astype(BF16)
    lo = (r1 - mid.astype(F32)).astype(BF16)
    return hi, mid, lo


def _select_sum(sel, x):
    hi, mid, lo = _split3(x)
    return _dot(sel, hi) + _dot(sel, mid) + _dot(sel, lo)


def _group_row_bcast(x, group, row):
    r, c = x.shape
    x3 = x.reshape(r // group, group, c)
    return jnp.broadcast_to(x3[:, row:row + 1, :], x3.shape).reshape(r, c)


def _gla_intra(q, k, b, sub, span):
    r, dk = q.shape
    n = r // sub
    q3 = q.reshape(n, sub, dk)
    k3 = k.reshape(n, sub, dk)
    b3 = b.reshape(n, sub, dk)
    rowi = lax.broadcasted_iota(jnp.int32, (n, sub, dk), 1)
    lane = lax.broadcasted_iota(jnp.int32, (n, sub, dk), 2)
    band0 = lax.broadcasted_iota(jnp.int32, (n, sub, dk), 0) * sub
    a3 = jnp.zeros((n, sub, dk), F32)
    for s in range(sub):
        bs = b3[:, s:s + 1, :]
        ks = k3[:, s:s + 1, :]
        e = jnp.exp(jnp.minimum(b3 - bs, 0.0))
        val = jnp.sum(q3 * e * ks, axis=-1, keepdims=True)
        a3 = jnp.where((lane == band0 + s) & (rowi >= s), val, a3)
    a = a3.reshape(r, dk)

    row = lax.broadcasted_iota(jnp.int32, (r, r), 0)
    col = lax.broadcasted_iota(jnp.int32, (r, r), 1)
    row1 = lax.broadcasted_iota(jnp.int32, (r, 1), 0)
    half = sub
    while half < span:
        group = 2 * half
        ref_b = _group_row_bcast(b, group, half - 1)
        upper = (row1 % group) >= half
        qh = jnp.where(upper, q * jnp.exp(jnp.minimum(b - ref_b, 0.0)), 0.0)
        kh = jnp.where(upper, 0.0, k * jnp.exp(jnp.minimum(ref_b - b, 0.0)))
        a_l = _dot_nt(qh.astype(BF16), kh.astype(BF16))
        a = a + jnp.where((row // group) == (col // group), a_l, 0.0)
        half = group
    return a


def _gla_head_out(o, gn, r):
    o = o * lax.rsqrt(jnp.mean(o * o, axis=-1, keepdims=True) + EPS) * gn
    return o * (r * jax.nn.sigmoid(r))


def _gla_prompt_kernel(q_ref, k_ref, v_ref, r_ref, la_ref, gn_ref, y_ref, sfin_ref, s_ref):
    i = pl.program_id(1)
    rows = q_ref.shape[0]

    @pl.when(i == 0)
    def _():
        s_ref[...] = jnp.zeros_like(s_ref)

    q = q_ref[...] * (GLA_DK ** -0.5)
    k = k_ref[...]
    v = v_ref[...].astype(BF16)
    la = la_ref[...]

    row = lax.broadcasted_iota(jnp.int32, (rows, rows), 0)
    col = lax.broadcasted_iota(jnp.int32, (rows, rows), 1)
    tri = (col <= row).astype(BF16)
    b = _select_sum(tri, la)
    b_last = b[rows - 1:rows, :]

    a = _gla_intra(q, k, b, GLA_SUB, rows)
    s_old = s_ref[...]
    o = _dot((q * jnp.exp(b)).astype(BF16), s_old.astype(BF16)) + _dot(a.astype(BF16), v)
    y_ref[...] = _gla_head_out(o, gn_ref[...], r_ref[...])

    kd_t = (k * jnp.exp(b_last - b)).T
    decay = jnp.exp(jnp.sum(la.T, axis=1, keepdims=True))
    s_new = decay * s_old + _dot(kd_t.astype(BF16), v)
    s_ref[...] = s_new

    @pl.when(i == pl.num_programs(1) - 1)
    def _():
        sfin_ref[0] = s_new


def _gla_prompt(z, la, gn):
    t = z.shape[0]
    rows = GLA_ROWS
    qk = GLA_DK
    vv = GLA_DV
    return pl.pallas_call(
        _gla_prompt_kernel,
        grid=(GLA_HEADS, t // rows),
        in_specs=[
            pl.BlockSpec((rows, qk), lambda h, i: (i, Z_GQ // qk + h)),
            pl.BlockSpec((rows, qk), lambda h, i: (i, Z_GK // qk + h)),
            pl.BlockSpec((rows, vv), lambda h, i: (i, Z_GV // vv + h)),
            pl.BlockSpec((rows, vv), lambda h, i: (i, Z_GR // vv + h)),
            pl.BlockSpec((rows, qk), lambda h, i: (i, h)),
            pl.BlockSpec((1, vv), lambda h, i: (0, h)),
        ],
        out_specs=[
            pl.BlockSpec((rows, vv), lambda h, i: (i, h)),
            pl.BlockSpec((1, qk, vv), lambda h, i: (h, 0, 0)),
        ],
        out_shape=[
            jax.ShapeDtypeStruct((t, GLA_WIDTH), F32),
            jax.ShapeDtypeStruct((GLA_HEADS, qk, vv), F32),
        ],
        scratch_shapes=[pltpu.VMEM((qk, vv), F32)],
        compiler_params=_cparams(("parallel", "arbitrary")),
        name="gla_prompt",
    )(z, z, z, z, la, gn)


def _gla_sample_kernel(q_ref, k_ref, v_ref, r_ref, la_ref, gn_ref, s0_ref, y_ref, sfin_ref):
    rows = q_ref.shape[0]
    n_seq = rows // SAMPLE_T
    q = q_ref[...] * (GLA_DK ** -0.5)
    k = k_ref[...]
    v = v_ref[...].astype(BF16)
    la = la_ref[...]

    row = lax.broadcasted_iota(jnp.int32, (rows, rows), 0)
    col = lax.broadcasted_iota(jnp.int32, (rows, rows), 1)
    same_seq = (row // SAMPLE_T) == (col // SAMPLE_T)
    tri = ((col <= row) & same_seq).astype(BF16)
    b = _select_sum(tri, la)
    b_last = _group_row_bcast(b, SAMPLE_T, SAMPLE_T - 1)

    a = _gla_intra(q, k, b, SAMPLE_T, SAMPLE_T)
    o = _dot(a.astype(BF16), v)

    qd = q * jnp.exp(b)
    kd_t = (k * jnp.exp(b_last - b)).T
    la_t = la.T
    row1 = lax.broadcasted_iota(jnp.int32, (rows, 1), 0)
    lane1 = lax.broadcasted_iota(jnp.int32, (1, rows), 1)
    for sq in range(n_seq):
        s0 = s0_ref[sq, 0]
        in_rows = (row1 // SAMPLE_T) == sq
        in_lanes = (lane1 // SAMPLE_T) == sq
        o = o + _dot(jnp.where(in_rows, qd, 0.0).astype(BF16), s0.astype(BF16))
        decay = jnp.exp(jnp.sum(jnp.where(in_lanes, la_t, 0.0), axis=1, keepdims=True))
        sfin_ref[sq, 0] = decay * s0 + _dot(jnp.where(in_lanes, kd_t, 0.0).astype(BF16), v)
    y_ref[...] = _gla_head_out(o, gn_ref[...], r_ref[...])


def _gla_sample(z, la, gn, state):
    m = z.shape[0]
    rows = GLA_ROWS
    n_seq = rows // SAMPLE_T
    qk = GLA_DK
    vv = GLA_DV
    return pl.pallas_call(
        _gla_sample_kernel,
        grid=(GLA_HEADS, m // rows),
        in_specs=[
            pl.BlockSpec((rows, qk), lambda h, i: (i, Z_GQ // qk + h)),
            pl.BlockSpec((rows, qk), lambda h, i: (i, Z_GK // qk + h)),
            pl.BlockSpec((rows, vv), lambda h, i: (i, Z_GV // vv + h)),
            pl.BlockSpec((rows, vv), lambda h, i: (i, Z_GR // vv + h)),
            pl.BlockSpec((rows, qk), lambda h, i: (i, h)),
            pl.BlockSpec((1, vv), lambda h, i: (0, h)),
            pl.BlockSpec((n_seq, 1, qk, vv), lambda h, i: (i, h, 0, 0)),
        ],
        out_specs=[
            pl.BlockSpec((rows, vv), lambda h, i: (i, h)),
            pl.BlockSpec((n_seq, 1, qk, vv), lambda h, i: (i, h, 0, 0)),
        ],
        out_shape=[
            jax.ShapeDtypeStruct((m, GLA_WIDTH), F32),
            jax.ShapeDtypeStruct(state.shape, F32),
        ],
        compiler_params=_cparams(("parallel", "parallel")),
        name="gla_sample",
    )(z, z, z, z, la, gn, state)


def _lambda_value(lam_ref, lam_init):
    l = lam_ref[...]
    s1 = jnp.sum(l[0:1] * l[1:2], axis=-1, keepdims=True)
    s2 = jnp.sum(l[2:3] * l[3:4], axis=-1, keepdims=True)
    return jnp.exp(s1) - jnp.exp(s2) + lam_init


def _diff_head_out(acc1, l1, acc2, l2, lam, gn, lam_init):
    o = acc1 / l1 - lam * (acc2 / l2)
    o = o * lax.rsqrt(jnp.mean(o * o, axis=-1, keepdims=True) + EPS)
    return o * gn * (1.0 - lam_init)


def _diff_prompt_kernel(qi_ref, ki_ref, lam_ref, q_ref, k_ref, vt_ref, gn_ref, y_ref,
                        q_s, m_s, l_s, acc_s, *, lam_init):
    h = pl.program_id(0)
    step_id = pl.program_id(1)
    qi = qi_ref[step_id]
    ki = ki_ref[step_id]
    bq = q_ref.shape[0]
    bk = k_ref.shape[0]

    @pl.when(ki == 0)
    def _():
        q_s[...] = (q_ref[...] * (DIFF_DK ** -0.5 * LOG2_E)).astype(BF16)
        m_s[...] = jnp.full_like(m_s, NEG_BIG)
        l_s[...] = jnp.zeros_like(l_s)
        acc_s[...] = jnp.zeros_like(acc_s)

    def step(masked):
        k = k_ref[...].astype(BF16)
        vt = vt_ref[...]
        head = jnp.full((bk, V7X_LANES), h + 1, jnp.int32).astype(F32)
        slope = jnp.exp2(-head * (8.0 / DIFF_HEADS)) * LOG2_E
        kpos = (ki * bk - qi * bq + lax.broadcasted_iota(jnp.int32, (bk, V7X_LANES), 0)).astype(F32)
        bias = jnp.tile(slope * kpos, (1, bq // V7X_LANES))
        if masked:
            row = lax.broadcasted_iota(jnp.int32, (bk, bq), 0)
            col = lax.broadcasted_iota(jnp.int32, (bk, bq), 1)
            keep = row <= col
        for mp in range(2):
            c = mp * DIFF_DK
            s = _dot_nt(k[:, c:c + DIFF_DK], q_s[:, c:c + DIFF_DK]) + bias
            if masked:
                s = jnp.where(keep, s, -jnp.inf)
            m_old = m_s[mp]
            m_new = jnp.maximum(m_old, jnp.max(s, axis=0, keepdims=True))
            alpha = jnp.exp2(m_old - m_new)
            p = jnp.exp2(s - m_new)
            l_s[mp] = alpha * l_s[mp] + jnp.sum(p, axis=0, keepdims=True)
            acc_s[mp] = alpha * acc_s[mp] + _dot(vt, p.astype(BF16))
            m_s[mp] = m_new

    @pl.when(ki < qi)
    def _():
        step(False)

    @pl.when(ki == qi)
    def _():
        step(True)
        lam = _lambda_value(lam_ref, lam_init)
        o = acc_s[0] / l_s[0] - lam * (acc_s[1] / l_s[1])
        o = o * lax.rsqrt(jnp.mean(o * o, axis=0, keepdims=True) + EPS)
        y_ref[...] = (o * gn_ref[...] * (1.0 - lam_init)).T


def _diff_prompt(z, vt, lam_vecs, gn_col, lam_init):
    t = z.shape[0]
    bq = _pick_block(t, 512)
    w = 2 * DIFF_DK
    nq = t // bq
    pairs = [(qi, ki) for qi in range(nq) for ki in range(qi + 1)]
    qi_arr = jnp.asarray([p[0] for p in pairs], jnp.int32)
    ki_arr = jnp.asarray([p[1] for p in pairs], jnp.int32)
    grid_spec = pltpu.PrefetchScalarGridSpec(
        num_scalar_prefetch=2,
        grid=(DIFF_HEADS, len(pairs)),
        in_specs=[
            pl.BlockSpec((4, DIFF_DK), lambda h, s, qi, ki: (0, 0)),
            pl.BlockSpec((bq, w), lambda h, s, qi, ki: (qi[s], Z_DQ // w + h)),
            pl.BlockSpec((bq, w), lambda h, s, qi, ki: (ki[s], Z_DK // w + h)),
            pl.BlockSpec((DIFF_DV, bq), lambda h, s, qi, ki: (h, ki[s])),
            pl.BlockSpec((DIFF_DV, 1), lambda h, s, qi, ki: (h, 0)),
        ],
        out_specs=pl.BlockSpec((bq, DIFF_DV), lambda h, s, qi, ki: (qi[s], h)),
        scratch_shapes=[
            pltpu.VMEM((bq, w), BF16),
            pltpu.VMEM((2, 1, bq), F32),
            pltpu.VMEM((2, 1, bq), F32),
            pltpu.VMEM((2, DIFF_DV, bq), F32),
        ],
    )
    return pl.pallas_call(
        functools.partial(_diff_prompt_kernel, lam_init=lam_init),
        grid_spec=grid_spec,
        out_shape=jax.ShapeDtypeStruct((t, DIFF_WIDTH), F32),
        compiler_params=_cparams(("parallel", "arbitrary")),
        name="diff_prompt",
    )(qi_arr, ki_arr, lam_vecs, z, z, vt, gn_col)


def _diff_sample_kernel(pt_ref, lam_ref, q_ref, kn_ref, vn_ref, *rest, lam_init, past_len, pps):
    del pt_ref
    kp_refs, vp_refs = rest[:pps], rest[pps:2 * pps]
    gn_ref, y_ref, q_s, bias_s, m_s, l_s, acc_s = rest[2 * pps:]
    p = pl.program_id(1)
    tn = q_ref.shape[0]
    page = kp_refs[0].shape[2]
    hq = DIFF_HEADS * tn
    n_rows = 2 * hq
    n_keys = page * DIFF_HEADS
    w = 2 * DIFF_DK

    row1 = lax.broadcasted_iota(jnp.int32, (n_rows, 1), 0)
    slope = jnp.exp2(-((row1 % hq) // tn + 1).astype(F32) * (8.0 / DIFF_HEADS))

    def attend(kflat, vflat, bias):
        s = jnp.concatenate(
            [_dot_nt(q_s[mp], kflat[:, mp * DIFF_DK:(mp + 1) * DIFF_DK]) for mp in range(2)],
            axis=0) + bias
        m_old = m_s[...]
        m_new = jnp.maximum(m_old, jnp.max(s, axis=-1, keepdims=True))
        alpha = jnp.exp(m_old - m_new)
        pr = jnp.exp(s - m_new)
        l_s[...] = alpha * l_s[...] + jnp.sum(pr, axis=-1, keepdims=True)
        m_s[...] = m_new
        acc_s[...] = alpha * acc_s[...] + _dot(pr.astype(BF16), vflat)

    @pl.when(p == 0)
    def _():
        q = q_ref[...] * (DIFF_DK ** -0.5)
        for mp in range(2):
            q_s[mp] = jnp.concatenate(
                [q[:, hh * w + mp * DIFF_DK:hh * w + (mp + 1) * DIFF_DK] for hh in range(DIFF_HEADS)],
                axis=0).astype(BF16)
        m_s[...] = jnp.full_like(m_s, NEG_BIG)
        l_s[...] = jnp.zeros_like(l_s)
        acc_s[...] = jnp.zeros_like(acc_s)

        row = lax.broadcasted_iota(jnp.int32, (n_rows, n_keys), 0)
        col = lax.broadcasted_iota(jnp.int32, (n_rows, n_keys), 1)
        same_head = (col % DIFF_HEADS) == ((row % hq) // tn)
        bias_s[...] = jnp.where(same_head, slope * (col // DIFF_HEADS).astype(F32), -jnp.inf)

        pad = jnp.zeros((V7X_LANES - hq, w), F32)
        k_new = jnp.concatenate(
            [kn_ref[:, hh * w:(hh + 1) * w] for hh in range(DIFF_HEADS)] + [pad], axis=0)
        v_new = jnp.concatenate(
            [vn_ref[:, hh * DIFF_DV:(hh + 1) * DIFF_DV] for hh in range(DIFF_HEADS)] + [pad], axis=0)
        rown = lax.broadcasted_iota(jnp.int32, (n_rows, V7X_LANES), 0)
        coln = lax.broadcasted_iota(jnp.int32, (n_rows, V7X_LANES), 1)
        visible = ((coln // tn) == ((rown % hq) // tn)) & ((coln % tn) <= (rown % tn))
        bias_new = jnp.where(visible, slope * (coln % tn).astype(F32), -jnp.inf)
        attend(k_new.astype(BF16), v_new.astype(BF16), bias_new)

    for j in range(pps):
        base = ((p * pps + j) * page - past_len).astype(F32)
        attend(kp_refs[j][0, 0].reshape(n_keys, w).astype(BF16),
               vp_refs[j][0, 0].reshape(n_keys, DIFF_DV).astype(BF16),
               bias_s[...] + slope * base)

    @pl.when(p == pl.num_programs(1) - 1)
    def _():
        lam = _lambda_value(lam_ref, lam_init)
        acc = acc_s[...]
        l = l_s[...]
        for hh in range(DIFF_HEADS):
            r1 = hh * tn
            r2 = hq + hh * tn
            y_ref[:, hh * DIFF_DV:(hh + 1) * DIFF_DV] = _diff_head_out(
                acc[r1:r1 + tn], l[r1:r1 + tn], acc[r2:r2 + tn], l[r2:r2 + tn],
                lam, gn_ref[:, hh * DIFF_DV:(hh + 1) * DIFF_DV], lam_init)


def _diff_sample(z, cache_k, cache_v, layer, page_table, lam_vecs, gn, lam_init):
    m = z.shape[0]
    tn = SAMPLE_T
    n_batch = m // tn
    n_pages = page_table.shape[1]
    page = cache_k.shape[2]
    pps = 4 if n_pages % 4 == 0 else (2 if n_pages % 2 == 0 else 1)
    pt = page_table.reshape(-1)
    n_rows = DIFF_HEADS * 2 * tn
    w = 2 * DIFF_DK

    def page_spec(width, j):
        return pl.BlockSpec(
            (1, 1, page, DIFF_HEADS, width),
            lambda b, p, pt: (layer, pt[b * n_pages + p * pps + j], 0, 0, 0))

    grid_spec = pltpu.PrefetchScalarGridSpec(
        num_scalar_prefetch=1,
        grid=(n_batch, n_pages // pps),
        in_specs=[
            pl.BlockSpec((4, DIFF_DK), lambda b, p, pt: (0, 0)),
            pl.BlockSpec((tn, DIFF_QK), lambda b, p, pt: (b, Z_DQ // DIFF_QK)),
            pl.BlockSpec((tn, DIFF_QK), lambda b, p, pt: (b, Z_DK // DIFF_QK)),
            pl.BlockSpec((tn, DIFF_WIDTH), lambda b, p, pt: (b, Z_DV // DIFF_WIDTH)),
        ] + [page_spec(w, j) for j in range(pps)] + [page_spec(DIFF_DV, j) for j in range(pps)] + [
            pl.BlockSpec((1, DIFF_WIDTH), lambda b, p, pt: (0, 0)),
        ],
        out_specs=pl.BlockSpec((tn, DIFF_WIDTH), lambda b, p, pt: (b, 0)),
        scratch_shapes=[
            pltpu.VMEM((2, DIFF_HEADS * tn, DIFF_DK), BF16),
            pltpu.VMEM((n_rows, page * DIFF_HEADS), F32),
            pltpu.VMEM((n_rows, 1), F32),
            pltpu.VMEM((n_rows, 1), F32),
            pltpu.VMEM((n_rows, DIFF_DV), F32),
        ],
    )
    return pl.pallas_call(
        functools.partial(_diff_sample_kernel, lam_init=lam_init, past_len=n_pages * page, pps=pps),
        grid_spec=grid_spec,
        out_shape=jax.ShapeDtypeStruct((m, DIFF_WIDTH), F32),
        compiler_params=_cparams(("parallel", "arbitrary")),
        name="diff_sample",
    )(pt, lam_vecs, z, z, z, *([cache_k] * pps), *([cache_v] * pps), gn)


def _prep_weights(w_in, w_alpha2, b_alpha, w_proj_a, w_proj_b, w_o, w_up, w_down):
    w_main = jnp.concatenate(
        [w_in[:, :LR_START], w_in[:, LR_START + GLA_RANK:]], axis=1).astype(BF16)
    w_lr = jnp.pad(w_in[:, LR_START:LR_START + GLA_RANK],
                   ((0, 0), (0, V7X_LANES - GLA_RANK))).astype(BF16)
    w_a2 = jnp.pad(w_alpha2, ((0, V7X_LANES - GLA_RANK), (0, 0))).astype(BF16)
    dv0 = LR_START + GLA_RANK + 2 * DIFF_QK
    w_dv_t = w_in[:, dv0:dv0 + DIFF_WIDTH].T.astype(BF16)
    return dict(w_main=w_main, w_lr=w_lr, w_a2=w_a2, b_a=b_alpha.reshape(1, GLA_KEY), w_dv_t=w_dv_t,
                w_pa=w_proj_a.astype(BF16), w_pb=w_proj_b.astype(BF16), w_o=w_o.astype(BF16),
                w_up=w_up.astype(BF16), w_down=w_down.astype(BF16))


def _mix_inputs(x, norm_g, wts):
    h, la = _norm_lr(x, norm_g.reshape(1, D_MODEL), wts["w_lr"], wts["w_a2"], wts["b_a"])
    z = _in_proj(h, wts["w_main"])
    return z, la, h


def _channel_mix(x, ya, yb, z, wts, norm_mlp):
    u = _merge(ya, yb, wts["w_pa"], wts["w_pb"], z)
    x2 = _oproj(u, wts["w_o"], x)
    h2 = _norm(x2, norm_mlp.reshape(1, D_MODEL))
    return x2, _mlp(h2, wts["w_up"], wts["w_down"])


def kernel(x_prompt, x_sample, cache_k, cache_v, state_gla, page_table, norm_mix, w_in, w_alpha2,
           b_alpha, gla_norm, lambda_q1, lambda_k1, lambda_q2, lambda_k2, diff_norm, w_proj_a,
           w_proj_b, w_o, norm_mlp, w_up, w_down, final_norm):
    depth = w_in.shape[0]
    bp, tp = x_prompt.shape[0], x_prompt.shape[1]
    bs, ts = x_sample.shape[0], x_sample.shape[1]
    assert bp == 1 and ts == SAMPLE_T
    xp = x_prompt.reshape(tp, D_MODEL)
    xs = x_sample.reshape(bs * ts, D_MODEL)
    fin = final_norm.reshape(1, D_MODEL)
    kp_l, vp_l, sp_l, ks_l, vs_l, ss_l = [], [], [], [], [], []
    for l in range(depth):
        lam_init = 0.8 - 0.6 * math.exp(-0.3 * l)
        wts = _prep_weights(w_in[l], w_alpha2[l], b_alpha[l], w_proj_a[l], w_proj_b[l], w_o[l],
                            w_up[l], w_down[l])
        lam_vecs = jnp.stack([lambda_q1[l], lambda_k1[l], lambda_q2[l], lambda_k2[l]]).astype(F32)
        gn_a = gla_norm[l].reshape(1, GLA_WIDTH)
        gn_b = diff_norm[l].reshape(1, DIFF_WIDTH)
        last = l == depth - 1

        z, la, h = _mix_inputs(xp, norm_mix[l], wts)
        ya, s_fin = _gla_prompt(z, la, gn_a)
        vt = _vt_proj(wts["w_dv_t"], h)
        yb = _diff_prompt(z, vt, lam_vecs, diff_norm[l].reshape(DIFF_WIDTH, 1), lam_init)
        x2, mlp = _channel_mix(xp, ya, yb, z, wts, norm_mlp[l])
        xp = _final(x2, mlp, fin) if last else x2 + mlp
        kp_l.append(z[:, Z_DK:Z_DK + DIFF_QK].reshape(bp, tp, DIFF_HEADS, 2 * DIFF_DK))
        vp_l.append(z[:, Z_DV:Z_DV + DIFF_WIDTH].reshape(bp, tp, DIFF_HEADS, DIFF_DV))
        sp_l.append(s_fin.reshape(bp, GLA_HEADS, GLA_DK, GLA_DV))

        z, la, _ = _mix_inputs(xs, norm_mix[l], wts)
        ya, s_fin = _gla_sample(z, la, gn_a, state_gla[l])
        yb = _diff_sample(z, cache_k, cache_v, l, page_table, lam_vecs, gn_b, lam_init)
        x2, mlp = _channel_mix(xs, ya, yb, z, wts, norm_mlp[l])
        xs = _final(x2, mlp, fin) if last else x2 + mlp
        ks_l.append(z[:, Z_DK:Z_DK + DIFF_QK].reshape(bs, ts, DIFF_HEADS, 2 * DIFF_DK))
        vs_l.append(z[:, Z_DV:Z_DV + DIFF_WIDTH].reshape(bs, ts, DIFF_HEADS, DIFF_DV))
        ss_l.append(s_fin)

    return (xp.reshape(bp, tp, D_MODEL), xs.reshape(bs, ts, D_MODEL), jnp.stack(kp_l),
            jnp.stack(vp_l), jnp.stack(sp_l), jnp.stack(ks_l), jnp.stack(vs_l), jnp.stack(ss_l))
```

```python
import functools
import math

import jax
import jax.numpy as jnp
from jax import lax
from jax.experimental import pallas as pl
from jax.experimental.pallas import tpu as pltpu

F32 = jnp.float32
BF16 = jnp.bfloat16

D_MODEL = 4096
PAGE_SIZE = 128
GLA_HEADS = 8
GLA_DK = 128
GLA_DV = 256
GLA_KEY = GLA_HEADS * GLA_DK
GLA_WIDTH = GLA_HEADS * GLA_DV
GLA_RANK = 16
GLA_TAU = 16.0
DIFF_HEADS = 8
DIFF_DK = 128
DIFF_DV = 256
DIFF_QK = DIFF_HEADS * 2 * DIFF_DK
DIFF_WIDTH = DIFF_HEADS * DIFF_DV
EPS = 1e-6

Z_GQ = 0
Z_GK = Z_GQ + GLA_KEY
Z_GV = Z_GK + GLA_KEY
Z_GR = Z_GV + GLA_WIDTH
Z_DQ = Z_GR + GLA_WIDTH
Z_DK = Z_DQ + DIFF_QK
Z_DV = Z_DK + DIFF_QK
Z_GA = Z_DV + DIFF_WIDTH
Z_GB = Z_GA + D_MODEL
Z_COLS = Z_GB + D_MODEL
LR_START = 2 * GLA_KEY + 2 * GLA_WIDTH

V7X_LANES = 128
V7X_SUBLANES = 8
V7X_VMEM_LIMIT_BYTES = 56 * 1024 * 1024

GLA_ROWS = 128
GLA_SUB = 8
GLA_STEP_HEADS = 4
FLASH_HEADS = 4
SAMPLE_T = 8
NEG_BIG = -1e30
LOG2_E = 1.4426950408889634


def _pick_block(n, pref, mult=V7X_LANES):
    b = (min(pref, n) // mult) * mult
    while b > mult and n % b:
        b -= mult
    assert b > 0 and n % b == 0, (n, pref, mult)
    return b


def _cparams(sem):
    return pltpu.CompilerParams(dimension_semantics=sem, vmem_limit_bytes=V7X_VMEM_LIMIT_BYTES)


def _dot(a, b):
    return jnp.dot(a, b, preferred_element_type=F32)


def _dot_nt(a, b):
    return lax.dot_general(a, b, (((1,), (1,)), ((), ())), preferred_element_type=F32)


def _norm_lr_kernel(x_ref, g_ref, wlr_ref, wa2_ref, ba_ref, h_ref, la_ref):
    x = x_ref[...]
    ms = jnp.mean(x * x, axis=-1, keepdims=True)
    h = (x * lax.rsqrt(ms + EPS) * g_ref[...]).astype(BF16)
    h_ref[...] = h
    g_lr = _dot(h, wlr_ref[...])
    xa = _dot(g_lr.astype(BF16), wa2_ref[...]) + ba_ref[...]
    la_ref[...] = jax.nn.log_sigmoid(xa) * (1.0 / GLA_TAU)


def _norm_lr(x, g, w_lr, w_a2, b_a):
    m = x.shape[0]
    bm = _pick_block(m, 256)
    return pl.pallas_call(
        _norm_lr_kernel,
        grid=(m // bm,),
        in_specs=[
            pl.BlockSpec((bm, D_MODEL), lambda i: (i, 0)),
            pl.BlockSpec((1, D_MODEL), lambda i: (0, 0)),
            pl.BlockSpec((D_MODEL, V7X_LANES), lambda i: (0, 0)),
            pl.BlockSpec((V7X_LANES, GLA_KEY), lambda i: (0, 0)),
            pl.BlockSpec((1, GLA_KEY), lambda i: (0, 0)),
        ],
        out_specs=[
            pl.BlockSpec((bm, D_MODEL), lambda i: (i, 0)),
            pl.BlockSpec((bm, GLA_KEY), lambda i: (i, 0)),
        ],
        out_shape=[
            jax.ShapeDtypeStruct((m, D_MODEL), BF16),
            jax.ShapeDtypeStruct((m, GLA_KEY), F32),
        ],
        compiler_params=_cparams(("parallel",)),
        name="norm_lr",
    )(x, g, w_lr, w_a2, b_a)


def _norm_kernel(x_ref, g_ref, h_ref):
    x = x_ref[...]
    ms = jnp.mean(x * x, axis=-1, keepdims=True)
    h_ref[...] = (x * lax.rsqrt(ms + EPS) * g_ref[...]).astype(h_ref.dtype)


def _norm(x, g):
    m = x.shape[0]
    bm = _pick_block(m, 256)
    return pl.pallas_call(
        _norm_kernel,
        grid=(m // bm,),
        in_specs=[
            pl.BlockSpec((bm, D_MODEL), lambda i: (i, 0)),
            pl.BlockSpec((1, D_MODEL), lambda i: (0, 0)),
        ],
        out_specs=pl.BlockSpec((bm, D_MODEL), lambda i: (i, 0)),
        out_shape=jax.ShapeDtypeStruct((m, D_MODEL), BF16),
        compiler_params=_cparams(("parallel",)),
        name="norm_mlp",
    )(x, g)


def _final_kernel(x_ref, m_ref, g_ref, y_ref):
    x = x_ref[...] + m_ref[...]
    ms = jnp.mean(x * x, axis=-1, keepdims=True)
    y_ref[...] = x * lax.rsqrt(ms + EPS) * g_ref[...]


def _final(x2, mlp, g):
    m = x2.shape[0]
    bm = _pick_block(m, 256)
    return pl.pallas_call(
        _final_kernel,
        grid=(m // bm,),
        in_specs=[
            pl.BlockSpec((bm, D_MODEL), lambda i: (i, 0)),
            pl.BlockSpec((bm, D_MODEL), lambda i: (i, 0)),
            pl.BlockSpec((1, D_MODEL), lambda i: (0, 0)),
        ],
        out_specs=pl.BlockSpec((bm, D_MODEL), lambda i: (i, 0)),
        out_shape=jax.ShapeDtypeStruct((m, D_MODEL), F32),
        compiler_params=_cparams(("parallel",)),
        name="final_norm",
    )(x2, mlp, g)


def _matmul_kernel(a_ref, w_ref, o_ref):
    o_ref[...] = _dot(a_ref[...], w_ref[...])


def _in_proj(h, w):
    m, k = h.shape
    n = w.shape[1]
    bm = _pick_block(m, 1024)
    bn = _pick_block(n, 1024)
    return pl.pallas_call(
        _matmul_kernel,
        grid=(m // bm, n // bn),
        in_specs=[
            pl.BlockSpec((bm, k), lambda i, j: (i, 0)),
            pl.BlockSpec((k, bn), lambda i, j: (0, j)),
        ],
        out_specs=pl.BlockSpec((bm, bn), lambda i, j: (i, j)),
        out_shape=jax.ShapeDtypeStruct((m, n), F32),
        compiler_params=_cparams(("parallel", "arbitrary")),
        name="in_proj",
    )(h, w)


def _vt_proj_kernel(w_ref, h_ref, o_ref):
    o_ref[...] = _dot_nt(w_ref[...], h_ref[...]).astype(BF16)


def _vt_proj(w_t, h):
    n, k = w_t.shape
    m = h.shape[0]
    bm = _pick_block(m, 1024)
    bn = _pick_block(n, 1024)
    return pl.pallas_call(
        _vt_proj_kernel,
        grid=(m // bm, n // bn),
        in_specs=[
            pl.BlockSpec((bn, k), lambda i, j: (j, 0)),
            pl.BlockSpec((bm, k), lambda i, j: (i, 0)),
        ],
        out_specs=pl.BlockSpec((bn, bm), lambda i, j: (j, i)),
        out_shape=jax.ShapeDtypeStruct((n, m), BF16),
        compiler_params=_cparams(("parallel", "arbitrary")),
        name="vt_proj",
    )(w_t, h)


def _merge_kernel(ya_ref, yb_ref, wa_ref, wb_ref, ga_ref, gb_ref, u_ref):
    pa = _dot(ya_ref[...].astype(BF16), wa_ref[...])
    pb = _dot(yb_ref[...].astype(BF16), wb_ref[...])
    u = jax.nn.sigmoid(ga_ref[...]) * pa + jax.nn.sigmoid(gb_ref[...]) * pb
    u_ref[...] = u.astype(BF16)


def _merge(ya, yb, w_pa, w_pb, z):
    m = ya.shape[0]
    bm = _pick_block(m, 512)
    bn = 1024
    ga0, gb0 = Z_GA // bn, Z_GB // bn
    return pl.pallas_call(
        _merge_kernel,
        grid=(m // bm, D_MODEL // bn),
        in_specs=[
            pl.BlockSpec((bm, GLA_WIDTH), lambda i, j: (i, 0)),
            pl.BlockSpec((bm, DIFF_WIDTH), lambda i, j: (i, 0)),
            pl.BlockSpec((GLA_WIDTH, bn), lambda i, j: (0, j)),
            pl.BlockSpec((DIFF_WIDTH, bn), lambda i, j: (0, j)),
            pl.BlockSpec((bm, bn), lambda i, j: (i, ga0 + j)),
            pl.BlockSpec((bm, bn), lambda i, j: (i, gb0 + j)),
        ],
        out_specs=pl.BlockSpec((bm, bn), lambda i, j: (i, j)),
        out_shape=jax.ShapeDtypeStruct((m, D_MODEL), BF16),
        compiler_params=_cparams(("parallel", "arbitrary")),
        name="merge",
    )(ya, yb, w_pa, w_pb, z, z)


def _oproj_kernel(u_ref, w_ref, x_ref, o_ref):
    o_ref[...] = x_ref[...] + _dot(u_ref[...], w_ref[...])


def _oproj(u, w_o, x):
    m = u.shape[0]
    bm = _pick_block(m, 1024)
    bn = 1024
    return pl.pallas_call(
        _oproj_kernel,
        grid=(m // bm, D_MODEL // bn),
        in_specs=[
            pl.BlockSpec((bm, D_MODEL), lambda i, j: (i, 0)),
            pl.BlockSpec((D_MODEL, bn), lambda i, j: (0, j)),
            pl.BlockSpec((bm, bn), lambda i, j: (i, j)),
        ],
        out_specs=pl.BlockSpec((bm, bn), lambda i, j: (i, j)),
        out_shape=jax.ShapeDtypeStruct((m, D_MODEL), F32),
        compiler_params=_cparams(("parallel", "arbitrary")),
        name="o_proj",
    )(u, w_o, x)


def _mlp_kernel(h_ref, wu_ref, wd_ref, o_ref, t0_s, t1_s):
    f = pl.program_id(1)
    last = pl.num_programs(1) - 1

    def up():
        return jnp.square(jnp.maximum(_dot(h_ref[...], wu_ref[...]), 0.0)).astype(BF16)

    @pl.when(f == 0)
    def _():
        t0_s[...] = up()

    @pl.when(f == 1)
    def _():
        o_ref[...] = _dot(t0_s[...], wd_ref[...])
        t1_s[...] = up()

    def middle(t_prev, t_next):
        o_ref[...] += _dot(t_prev[...], wd_ref[...])
        t_next[...] = up()

    @pl.when((f > 1) & (f < last) & (f % 2 == 0))
    def _():
        middle(t1_s, t0_s)

    @pl.when((f > 1) & (f < last) & (f % 2 == 1))
    def _():
        middle(t0_s, t1_s)

    @pl.when((f == last) & (f % 2 == 0))
    def _():
        o_ref[...] += _dot(t1_s[...], wd_ref[...])

    @pl.when((f == last) & (f % 2 == 1))
    def _():
        o_ref[...] += _dot(t0_s[...], wd_ref[...])


def _mlp(h2, w_up, w_down):
    m = h2.shape[0]
    d_ff = w_up.shape[1]
    bm = _pick_block(m, 512)
    bf = _pick_block(d_ff, 512)
    nf = d_ff // bf
    assert nf >= 2
    return pl.pallas_call(
        _mlp_kernel,
        grid=(m // bm, nf + 1),
        in_specs=[
            pl.BlockSpec((bm, D_MODEL), lambda i, f: (i, 0)),
            pl.BlockSpec((D_MODEL, bf), lambda i, f: (0, jnp.minimum(f, nf - 1))),
            pl.BlockSpec((bf, D_MODEL), lambda i, f: (jnp.maximum(f - 1, 0), 0)),
        ],
        out_specs=pl.BlockSpec((bm, D_MODEL), lambda i, f: (i, 0)),
        out_shape=jax.ShapeDtypeStruct((m, D_MODEL), F32),
        scratch_shapes=[pltpu.VMEM((bm, bf), BF16), pltpu.VMEM((bm, bf), BF16)],
        compiler_params=_cparams(("parallel", "arbitrary")),
        name="mlp",
    )(h2, w_up, w_down)


def _split3(x):
    hi = x.astype(BF16)
    r1 = x - hi.astype(F32)
    mid = r1.astype(BF16)
    lo = (r1 - mid.astype(F32)).astype(BF16)
    return hi, mid, lo


def _select_sum(sel, x):
    hi, mid, lo = _split3(x)
    return _dot(sel, hi) + _dot(sel, mid) + _dot(sel, lo)


def _group_row_bcast(x, group, row):
    r, c = x.shape
    x3 = x.reshape(r // group, group, c)
    return jnp.broadcast_to(x3[:, row:row + 1, :], x3.shape).reshape(r, c)


def _gla_intra(q, k, b, sub, span):
    r, dk = q.shape
    n = r // sub
    q3 = q.reshape(n, sub, dk)
    k3 = k.reshape(n, sub, dk)
    b3 = b.reshape(n, sub, dk)
    rowi = lax.broadcasted_iota(jnp.int32, (n, sub, dk), 1)
    lane = lax.broadcasted_iota(jnp.int32, (n, sub, dk), 2)
    band0 = lax.broadcasted_iota(jnp.int32, (n, sub, dk), 0) * sub
    a3 = jnp.zeros((n, sub, dk), F32)
    for s in range(sub):
        bs = b3[:, s:s + 1, :]
        ks = k3[:, s:s + 1, :]
        e = jnp.exp(jnp.minimum(b3 - bs, 0.0))
        val = jnp.sum(q3 * e * ks, axis=-1, keepdims=True)
        a3 = jnp.where((lane == band0 + s) & (rowi >= s), val, a3)
    a = a3.reshape(r, dk)

    row = lax.broadcasted_iota(jnp.int32, (r, r), 0)
    col = lax.broadcasted_iota(jnp.int32, (r, r), 1)
    row1 = lax.broadcasted_iota(jnp.int32, (r, 1), 0)
    half = sub
    while half < span:
        group = 2 * half
        ref_b = _group_row_bcast(b, group, half - 1)
        upper = (row1 % group) >= half
        qh = jnp.where(upper, q * jnp.exp(jnp.minimum(b - ref_b, 0.0)), 0.0)
        kh = jnp.where(upper, 0.0, k * jnp.exp(jnp.minimum(ref_b - b, 0.0)))
        a_l = _dot_nt(qh.astype(BF16), kh.astype(BF16))
        a = a + jnp.where((row // group) == (col // group), a_l, 0.0)
        half = group
    return a


def _gla_head_out(o, gn, r):
    o = o * lax.rsqrt(jnp.mean(o * o, axis=-1, keepdims=True) + EPS) * gn
    return o * (r * jax.nn.sigmoid(r))


def _gla_prompt_kernel(q_ref, k_ref, v_ref, r_ref, la_ref, gn_ref, y_ref, sfin_ref, s_ref):
    i = pl.program_id(1)
    rows = q_ref.shape[0]

    @pl.when(i == 0)
    def _():
        s_ref[...] = jnp.zeros_like(s_ref)

    row = lax.broadcasted_iota(jnp.int32, (rows, rows), 0)
    col = lax.broadcasted_iota(jnp.int32, (rows, rows), 1)
    tri = (col <= row).astype(BF16)
    for hh in range(GLA_STEP_HEADS):
        ck = slice(hh * GLA_DK, (hh + 1) * GLA_DK)
        cv = slice(hh * GLA_DV, (hh + 1) * GLA_DV)
        q = q_ref[:, ck] * (GLA_DK ** -0.5)
        k = k_ref[:, ck]
        v = v_ref[:, cv].astype(BF16)
        la = la_ref[:, ck]

        b = _select_sum(tri, la)
        b_last = b[rows - 1:rows, :]

        a = _gla_intra(q, k, b, GLA_SUB, rows)
        s_old = s_ref[hh]
        o = _dot((q * jnp.exp(b)).astype(BF16), s_old.astype(BF16)) + _dot(a.astype(BF16), v)
        y_ref[:, cv] = _gla_head_out(o, gn_ref[:, cv], r_ref[:, cv])

        kd_t = (k * jnp.exp(b_last - b)).T
        decay = jnp.exp(jnp.sum(la.T, axis=1, keepdims=True))
        s_ref[hh] = decay * s_old + _dot(kd_t.astype(BF16), v)

    @pl.when(i == pl.num_programs(1) - 1)
    def _():
        sfin_ref[...] = s_ref[...]


def _gla_prompt(z, la, gn):
    t = z.shape[0]
    rows = GLA_ROWS
    nh = GLA_STEP_HEADS
    qk = GLA_DK * nh
    vv = GLA_DV * nh
    return pl.pallas_call(
        _gla_prompt_kernel,
        grid=(GLA_HEADS // nh, t // rows),
        in_specs=[
            pl.BlockSpec((rows, qk), lambda h, i: (i, Z_GQ // qk + h)),
            pl.BlockSpec((rows, qk), lambda h, i: (i, Z_GK // qk + h)),
            pl.BlockSpec((rows, vv), lambda h, i: (i, Z_GV // vv + h)),
            pl.BlockSpec((rows, vv), lambda h, i: (i, Z_GR // vv + h)),
            pl.BlockSpec((rows, qk), lambda h, i: (i, h)),
            pl.BlockSpec((1, vv), lambda h, i: (0, h)),
        ],
        out_specs=[
            pl.BlockSpec((rows, vv), lambda h, i: (i, h)),
            pl.BlockSpec((nh, GLA_DK, GLA_DV), lambda h, i: (h, 0, 0)),
        ],
        out_shape=[
            jax.ShapeDtypeStruct((t, GLA_WIDTH), F32),
            jax.ShapeDtypeStruct((GLA_HEADS, GLA_DK, GLA_DV), F32),
        ],
        scratch_shapes=[pltpu.VMEM((nh, GLA_DK, GLA_DV), F32)],
        compiler_params=_cparams(("parallel", "arbitrary")),
        name="gla_prompt",
    )(z, z, z, z, la, gn)


def _gla_sample_kernel(q_ref, k_ref, v_ref, r_ref, la_ref, gn_ref, s0_ref, y_ref, sfin_ref):
    rows = q_ref.shape[0]
    n_seq = rows // SAMPLE_T
    q = q_ref[...] * (GLA_DK ** -0.5)
    k = k_ref[...]
    v = v_ref[...].astype(BF16)
    la = la_ref[...]

    row = lax.broadcasted_iota(jnp.int32, (rows, rows), 0)
    col = lax.broadcasted_iota(jnp.int32, (rows, rows), 1)
    same_seq = (row // SAMPLE_T) == (col // SAMPLE_T)
    tri = ((col <= row) & same_seq).astype(BF16)
    b = _select_sum(tri, la)
    b_last = _group_row_bcast(b, SAMPLE_T, SAMPLE_T - 1)

    a = _gla_intra(q, k, b, SAMPLE_T, SAMPLE_T)
    o = _dot(a.astype(BF16), v)

    qd = q * jnp.exp(b)
    kd_t = (k * jnp.exp(b_last - b)).T
    la_t = la.T
    row1 = lax.broadcasted_iota(jnp.int32, (rows, 1), 0)
    lane1 = lax.broadcasted_iota(jnp.int32, (1, rows), 1)
    for sq in range(n_seq):
        s0 = s0_ref[sq, 0]
        in_rows = (row1 // SAMPLE_T) == sq
        in_lanes = (lane1 // SAMPLE_T) == sq
        o = o + _dot(jnp.where(in_rows, qd, 0.0).astype(BF16), s0.astype(BF16))
        decay = jnp.exp(jnp.sum(jnp.where(in_lanes, la_t, 0.0), axis=1, keepdims=True))
        sfin_ref[sq, 0] = decay * s0 + _dot(jnp.where(in_lanes, kd_t, 0.0).astype(BF16), v)
    y_ref[...] = _gla_head_out(o, gn_ref[...], r_ref[...])


def _gla_sample(z, la, gn, state):
    m = z.shape[0]
    rows = GLA_ROWS
    n_seq = rows // SAMPLE_T
    qk = GLA_DK
    vv = GLA_DV
    return pl.pallas_call(
        _gla_sample_kernel,
        grid=(GLA_HEADS, m // rows),
        in_specs=[
            pl.BlockSpec((rows, qk), lambda h, i: (i, Z_GQ // qk + h)),
            pl.BlockSpec((rows, qk), lambda h, i: (i, Z_GK // qk + h)),
            pl.BlockSpec((rows, vv), lambda h, i: (i, Z_GV // vv + h)),
            pl.BlockSpec((rows, vv), lambda h, i: (i, Z_GR // vv + h)),
            pl.BlockSpec((rows, qk), lambda h, i: (i, h)),
            pl.BlockSpec((1, vv), lambda h, i: (0, h)),
            pl.BlockSpec((n_seq, 1, qk, vv), lambda h, i: (i, h, 0, 0)),
        ],
        out_specs=[
            pl.BlockSpec((rows, vv), lambda h, i: (i, h)),
            pl.BlockSpec((n_seq, 1, qk, vv), lambda h, i: (i, h, 0, 0)),
        ],
        out_shape=[
            jax.ShapeDtypeStruct((m, GLA_WIDTH), F32),
            jax.ShapeDtypeStruct(state.shape, F32),
        ],
        compiler_params=_cparams(("parallel", "parallel")),
        name="gla_sample",
    )(z, z, z, z, la, gn, state)


def _lambda_value(lam_ref, lam_init):
    l = lam_ref[...]
    s1 = jnp.sum(l[0:1] * l[1:2], axis=-1, keepdims=True)
    s2 = jnp.sum(l[2:3] * l[3:4], axis=-1, keepdims=True)
    return jnp.exp(s1) - jnp.exp(s2) + lam_init


def _diff_head_out(acc1, l1, acc2, l2, lam, gn, lam_init):
    o = acc1 / l1 - lam * (acc2 / l2)
    o = o * lax.rsqrt(jnp.mean(o * o, axis=-1, keepdims=True) + EPS)
    return o * gn * (1.0 - lam_init)


def _diff_prompt_kernel(qi_ref, ki_ref, lam_ref, q_ref, k_ref, vt_ref, gn_ref, y_ref,
                        q_s, m_s, l_s, acc_s, *, lam_init):
    hg = pl.program_id(0)
    step_id = pl.program_id(1)
    qi = qi_ref[step_id]
    ki = ki_ref[step_id]
    bq = q_ref.shape[0]
    bk = k_ref.shape[0]
    w = 2 * DIFF_DK

    @pl.when(ki == 0)
    def _():
        q_s[...] = (q_ref[...] * (DIFF_DK ** -0.5 * LOG2_E)).astype(BF16)
        m_s[...] = jnp.full_like(m_s, NEG_BIG)
        l_s[...] = jnp.zeros_like(l_s)
        acc_s[...] = jnp.zeros_like(acc_s)

    def step(masked):
        kpos = (ki * bk - qi * bq + lax.broadcasted_iota(jnp.int32, (bk, V7X_LANES), 0)).astype(F32)
        if masked:
            row = lax.broadcasted_iota(jnp.int32, (bk, bq), 0)
            col = lax.broadcasted_iota(jnp.int32, (bk, bq), 1)
            keep = row <= col
        for hh in range(FLASH_HEADS):
            k = k_ref[:, hh * w:(hh + 1) * w].astype(BF16)
            vt = vt_ref[hh * DIFF_DV:(hh + 1) * DIFF_DV, :]
            head = jnp.full((bk, V7X_LANES), hg * FLASH_HEADS + hh + 1, jnp.int32).astype(F32)
            slope = jnp.exp2(-head * (8.0 / DIFF_HEADS)) * LOG2_E
            bias = jnp.tile(slope * kpos, (1, bq // V7X_LANES))
            for mp in range(2):
                c = hh * w + mp * DIFF_DK
                idx = hh * 2 + mp
                s = _dot_nt(k[:, mp * DIFF_DK:(mp + 1) * DIFF_DK], q_s[:, c:c + DIFF_DK]) + bias
                if masked:
                    s = jnp.where(keep, s, -jnp.inf)
                m_old = m_s[idx]
                m_new = jnp.maximum(m_old, jnp.max(s, axis=0, keepdims=True))
                alpha = jnp.exp2(m_old - m_new)
                p = jnp.exp2(s - m_new)
                l_s[idx] = alpha * l_s[idx] + jnp.sum(p, axis=0, keepdims=True)
                acc_s[idx] = alpha * acc_s[idx] + _dot(vt, p.astype(BF16))
                m_s[idx] = m_new

    @pl.when(ki < qi)
    def _():
        step(False)

    @pl.when(ki == qi)
    def _():
        step(True)
        lam = _lambda_value(lam_ref, lam_init)
        for hh in range(FLASH_HEADS):
            i1, i2 = hh * 2, hh * 2 + 1
            o = acc_s[i1] / l_s[i1] - lam * (acc_s[i2] / l_s[i2])
            o = o * lax.rsqrt(jnp.mean(o * o, axis=0, keepdims=True) + EPS)
            gn = gn_ref[hh * DIFF_DV:(hh + 1) * DIFF_DV, :]
            y_ref[:, hh * DIFF_DV:(hh + 1) * DIFF_DV] = (o * gn * (1.0 - lam_init)).T


def _diff_prompt(z, vt, lam_vecs, gn_col, lam_init):
    t = z.shape[0]
    bq = _pick_block(t, 512)
    w = 2 * DIFF_DK * FLASH_HEADS
    dvw = DIFF_DV * FLASH_HEADS
    nq = t // bq
    pairs = [(qi, ki) for qi in range(nq) for ki in range(qi + 1)]
    qi_arr = jnp.asarray([p[0] for p in pairs], jnp.int32)
    ki_arr = jnp.asarray([p[1] for p in pairs], jnp.int32)
    grid_spec = pltpu.PrefetchScalarGridSpec(
        num_scalar_prefetch=2,
        grid=(DIFF_HEADS // FLASH_HEADS, len(pairs)),
        in_specs=[
            pl.BlockSpec((4, DIFF_DK), lambda h, s, qi, ki: (0, 0)),
            pl.BlockSpec((bq, w), lambda h, s, qi, ki: (qi[s], Z_DQ // w + h)),
            pl.BlockSpec((bq, w), lambda h, s, qi, ki: (ki[s], Z_DK // w + h)),
            pl.BlockSpec((dvw, bq), lambda h, s, qi, ki: (h, ki[s])),
            pl.BlockSpec((dvw, 1), lambda h, s, qi, ki: (h, 0)),
        ],
        out_specs=pl.BlockSpec((bq, dvw), lambda h, s, qi, ki: (qi[s], h)),
        scratch_shapes=[
            pltpu.VMEM((bq, w), BF16),
            pltpu.VMEM((2 * FLASH_HEADS, 1, bq), F32),
            pltpu.VMEM((2 * FLASH_HEADS, 1, bq), F32),
            pltpu.VMEM((2 * FLASH_HEADS, DIFF_DV, bq), F32),
        ],
    )
    return pl.pallas_call(
        functools.partial(_diff_prompt_kernel, lam_init=lam_init),
        grid_spec=grid_spec,
        out_shape=jax.ShapeDtypeStruct((t, DIFF_WIDTH), F32),
        compiler_params=_cparams(("parallel", "arbitrary")),
        name="diff_prompt",
    )(qi_arr, ki_arr, lam_vecs, z, z, vt, gn_col)


def _diff_sample_kernel(pt_ref, lam_ref, q_ref, kn_ref, vn_ref, *rest, lam_init, past_len, pps):
    del pt_ref
    kp_refs, vp_refs = rest[:pps], rest[pps:2 * pps]
    gn_ref, y_ref, q_s, bias_s, m_s, l_s, acc_s = rest[2 * pps:]
    p = pl.program_id(1)
    tn = q_ref.shape[0]
    page = kp_refs[0].shape[2]
    hq = DIFF_HEADS * tn
    n_rows = 2 * hq
    n_keys = page * DIFF_HEADS
    w = 2 * DIFF_DK

    row1 = lax.broadcasted_iota(jnp.int32, (n_rows, 1), 0)
    slope = jnp.exp2(-((row1 % hq) // tn + 1).astype(F32) * (8.0 / DIFF_HEADS))

    def attend(kflat, vflat, bias):
        s = jnp.concatenate(
            [_dot_nt(q_s[mp], kflat[:, mp * DIFF_DK:(mp + 1) * DIFF_DK]) for mp in range(2)],
            axis=0) + bias
        m_old = m_s[...]
        m_new = jnp.maximum(m_old, jnp.max(s, axis=-1, keepdims=True))
        alpha = jnp.exp(m_old - m_new)
        pr = jnp.exp(s - m_new)
        l_s[...] = alpha * l_s[...] + jnp.sum(pr, axis=-1, keepdims=True)
        m_s[...] = m_new
        acc_s[...] = alpha * acc_s[...] + _dot(pr.astype(BF16), vflat)

    @pl.when(p == 0)
    def _():
        q = q_ref[...] * (DIFF_DK ** -0.5)
        for mp in range(2):
            q_s[mp] = jnp.concatenate(
                [q[:, hh * w + mp * DIFF_DK:hh * w + (mp + 1) * DIFF_DK] for hh in range(DIFF_HEADS)],
                axis=0).astype(BF16)
        m_s[...] = jnp.full_like(m_s, NEG_BIG)
        l_s[...] = jnp.zeros_like(l_s)
        acc_s[...] = jnp.zeros_like(acc_s)

        row = lax.broadcasted_iota(jnp.int32, (n_rows, n_keys), 0)
        col = lax.broadcasted_iota(jnp.int32, (n_rows, n_keys), 1)
        same_head = (col % DIFF_HEADS) == ((row % hq) // tn)
        bias_s[...] = jnp.where(same_head, slope * (col // DIFF_HEADS).astype(F32), -jnp.inf)

        pad = jnp.zeros((V7X_LANES - hq, w), F32)
        k_new = jnp.concatenate(
            [kn_ref[:, hh * w:(hh + 1) * w] for hh in range(DIFF_HEADS)] + [pad], axis=0)
        v_new = jnp.concatenate(
            [vn_ref[:, hh * DIFF_DV:(hh + 1) * DIFF_DV] for hh in range(DIFF_HEADS)] + [pad], axis=0)
        rown = lax.broadcasted_iota(jnp.int32, (n_rows, V7X_LANES), 0)
        coln = lax.broadcasted_iota(jnp.int32, (n_rows, V7X_LANES), 1)
        visible = ((coln // tn) == ((rown % hq) // tn)) & ((coln % tn) <= (rown % tn))
        bias_new = jnp.where(visible, slope * (coln % tn).astype(F32), -jnp.inf)
        attend(k_new.astype(BF16), v_new.astype(BF16), bias_new)

    for j in range(pps):
        base = ((p * pps + j) * page - past_len).astype(F32)
        attend(kp_refs[j][0, 0].reshape(n_keys, w).astype(BF16),
               vp_refs[j][0, 0].reshape(n_keys, DIFF_DV).astype(BF16),
               bias_s[...] + slope * base)

    @pl.when(p == pl.num_programs(1) - 1)
    def _():
        lam = _lambda_value(lam_ref, lam_init)
        acc = acc_s[...]
        l = l_s[...]
        for hh in range(DIFF_HEADS):
            r1 = hh * tn
            r2 = hq + hh * tn
            y_ref[:, hh * DIFF_DV:(hh + 1) * DIFF_DV] = _diff_head_out(
                acc[r1:r1 + tn], l[r1:r1 + tn], acc[r2:r2 + tn], l[r2:r2 + tn],
                lam, gn_ref[:, hh * DIFF_DV:(hh + 1) * DIFF_DV], lam_init)


def _diff_sample(z, cache_k, cache_v, layer, page_table, lam_vecs, gn, lam_init):
    m = z.shape[0]
    tn = SAMPLE_T
    n_batch = m // tn
    n_pages = page_table.shape[1]
    page = cache_k.shape[2]
    pps = next(c for c in (8, 4, 2, 1) if n_pages % c == 0)
    pt = page_table.reshape(-1)
    n_rows = DIFF_HEADS * 2 * tn
    w = 2 * DIFF_DK

    def page_spec(width, j):
        return pl.BlockSpec(
            (1, 1, page, DIFF_HEADS, width),
            lambda b, p, pt: (layer, pt[b * n_pages + p * pps + j], 0, 0, 0))

    grid_spec = pltpu.PrefetchScalarGridSpec(
        num_scalar_prefetch=1,
        grid=(n_batch, n_pages // pps),
        in_specs=[
            pl.BlockSpec((4, DIFF_DK), lambda b, p, pt: (0, 0)),
            pl.BlockSpec((tn, DIFF_QK), lambda b, p, pt: (b, Z_DQ // DIFF_QK)),
            pl.BlockSpec((tn, DIFF_QK), lambda b, p, pt: (b, Z_DK // DIFF_QK)),
            pl.BlockSpec((tn, DIFF_WIDTH), lambda b, p, pt: (b, Z_DV // DIFF_WIDTH)),
        ] + [page_spec(w, j) for j in range(pps)] + [page_spec(DIFF_DV, j) for j in range(pps)] + [
            pl.BlockSpec((1, DIFF_WIDTH), lambda b, p, pt: (0, 0)),
        ],
        out_specs=pl.BlockSpec((tn, DIFF_WIDTH), lambda b, p, pt: (b, 0)),
        scratch_shapes=[
            pltpu.VMEM((2, DIFF_HEADS * tn, DIFF_DK), BF16),
            pltpu.VMEM((n_rows, page * DIFF_HEADS), F32),
            pltpu.VMEM((n_rows, 1), F32),
            pltpu.VMEM((n_rows, 1), F32),
            pltpu.VMEM((n_rows, DIFF_DV), F32),
        ],
    )
    return pl.pallas_call(
        functools.partial(_diff_sample_kernel, lam_init=lam_init, past_len=n_pages * page, pps=pps),
        grid_spec=grid_spec,
        out_shape=jax.ShapeDtypeStruct((m, DIFF_WIDTH), F32),
        compiler_params=_cparams(("parallel", "arbitrary")),
        name="diff_sample",
    )(pt, lam_vecs, z, z, z, *([cache_k] * pps), *([cache_v] * pps), gn)


def _prep_weights(w_in, w_alpha2, b_alpha, w_proj_a, w_proj_b, w_o, w_up, w_down):
    w_main = jnp.concatenate(
        [w_in[:, :LR_START], w_in[:, LR_START + GLA_RANK:]], axis=1).astype(BF16)
    w_lr = jnp.pad(w_in[:, LR_START:LR_START + GLA_RANK],
                   ((0, 0), (0, V7X_LANES - GLA_RANK))).astype(BF16)
    w_a2 = jnp.pad(w_alpha2, ((0, V7X_LANES - GLA_RANK), (0, 0))).astype(BF16)
    dv0 = LR_START + GLA_RANK + 2 * DIFF_QK
    w_dv_t = w_in[:, dv0:dv0 + DIFF_WIDTH].T.astype(BF16)
    return dict(w_main=w_main, w_lr=w_lr, w_a2=w_a2, b_a=b_alpha.reshape(1, GLA_KEY), w_dv_t=w_dv_t,
                w_pa=w_proj_a.astype(BF16), w_pb=w_proj_b.astype(BF16), w_o=w_o.astype(BF16),
                w_up=w_up.astype(BF16), w_down=w_down.astype(BF16))


def _mix_inputs(x, norm_g, wts):
    h, la = _norm_lr(x, norm_g.reshape(1, D_MODEL), wts["w_lr"], wts["w_a2"], wts["b_a"])
    z = _in_proj(h, wts["w_main"])
    return z, la, h


def _channel_mix(x, ya, yb, z, wts, norm_mlp):
    u = _merge(ya, yb, wts["w_pa"], wts["w_pb"], z)
    x2 = _oproj(u, wts["w_o"], x)
    h2 = _norm(x2, norm_mlp.reshape(1, D_MODEL))
    return x2, _mlp(h2, wts["w_up"], wts["w_down"])


def kernel(x_prompt, x_sample, cache_k, cache_v, state_gla, page_table, norm_mix, w_in, w_alpha2,
           b_alpha, gla_norm, lambda_q1, lambda_k1, lambda_q2, lambda_k2, diff_norm, w_proj_a,
           w_proj_b, w_o, norm_mlp, w_up, w_down, final_norm):
    depth = w_in.shape[0]
    bp, tp = x_prompt.shape[0], x_prompt.shape[1]
    bs, ts = x_sample.shape[0], x_sample.shape[1]
    assert bp == 1 and ts == SAMPLE_T
    xp = x_prompt.reshape(tp, D_MODEL)
    xs = x_sample.reshape(bs * ts, D_MODEL)
    fin = final_norm.reshape(1, D_MODEL)
    kp_l, vp_l, sp_l, ks_l, vs_l, ss_l = [], [], [], [], [], []
    for l in range(depth):
        lam_init = 0.8 - 0.6 * math.exp(-0.3 * l)
        wts = _prep_weights(w_in[l], w_alpha2[l], b_alpha[l], w_proj_a[l], w_proj_b[l], w_o[l],
                            w_up[l], w_down[l])
        lam_vecs = jnp.stack([lambda_q1[l], lambda_k1[l], lambda_q2[l], lambda_k2[l]]).astype(F32)
        gn_a = gla_norm[l].reshape(1, GLA_WIDTH)
        gn_b = diff_norm[l].reshape(1, DIFF_WIDTH)
        last = l == depth - 1

        z, la, h = _mix_inputs(xp, norm_mix[l], wts)
        ya, s_fin = _gla_prompt(z, la, gn_a)
        vt = _vt_proj(wts["w_dv_t"], h)
        yb = _diff_prompt(z, vt, lam_vecs, diff_norm[l].reshape(DIFF_WIDTH, 1), lam_init)
        x2, mlp = _channel_mix(xp, ya, yb, z, wts, norm_mlp[l])
        xp = _final(x2, mlp, fin) if last else x2 + mlp
        kp_l.append(z[:, Z_DK:Z_DK + DIFF_QK].reshape(bp, tp, DIFF_HEADS, 2 * DIFF_DK))
        vp_l.append(z[:, Z_DV:Z_DV + DIFF_WIDTH].reshape(bp, tp, DIFF_HEADS, DIFF_DV))
        sp_l.append(s_fin.reshape(bp, GLA_HEADS, GLA_DK, GLA_DV))

        z, la, _ = _mix_inputs(xs, norm_mix[l], wts)
        ya, s_fin = _gla_sample(z, la, gn_a, state_gla[l])
        yb = _diff_sample(z, cache_k, cache_v, l, page_table, lam_vecs, gn_b, lam_init)
        x2, mlp = _channel_mix(xs, ya, yb, z, wts, norm_mlp[l])
        xs = _final(x2, mlp, fin) if last else x2 + mlp
        ks_l.append(z[:, Z_DK:Z_DK + DIFF_QK].reshape(bs, ts, DIFF_HEADS, 2 * DIFF_DK))
        vs_l.append(z[:, Z_DV:Z_DV + DIFF_WIDTH].reshape(bs, ts, DIFF_HEADS, DIFF_DV))
        ss_l.append(s_fin)

    return (xp.reshape(bp, tp, D_MODEL), xs.reshape(bs, ts, D_MODEL), jnp.stack(kp_l),
            jnp.stack(vp_l), jnp.stack(sp_l), jnp.stack(ks_l), jnp.stack(vs_l), jnp.stack(ss_l))
```

```python
import functools
import math

import jax
import jax.numpy as jnp
from jax import lax
from jax.experimental import pallas as pl
from jax.experimental.pallas import tpu as pltpu

F32 = jnp.float32
BF16 = jnp.bfloat16

D_MODEL = 4096
PAGE_SIZE = 128
GLA_HEADS = 8
GLA_DK = 128
GLA_DV = 256
GLA_KEY = GLA_HEADS * GLA_DK
GLA_WIDTH = GLA_HEADS * GLA_DV
GLA_RANK = 16
GLA_TAU = 16.0
DIFF_HEADS = 8
DIFF_DK = 128
DIFF_DV = 256
DIFF_QK = DIFF_HEADS * 2 * DIFF_DK
DIFF_WIDTH = DIFF_HEADS * DIFF_DV
EPS = 1e-6

Z_GQ = 0
Z_GK = Z_GQ + GLA_KEY
Z_GV = Z_GK + GLA_KEY
Z_GR = Z_GV + GLA_WIDTH
Z_DQ = Z_GR + GLA_WIDTH
Z_DK = Z_DQ + DIFF_QK
Z_DV = Z_DK + DIFF_QK
Z_GA = Z_DV + DIFF_WIDTH
Z_GB = Z_GA + D_MODEL
Z_COLS = Z_GB + D_MODEL
LR_START = 2 * GLA_KEY + 2 * GLA_WIDTH

V7X_LANES = 128
V7X_SUBLANES = 8
V7X_VMEM_LIMIT_BYTES = 56 * 1024 * 1024

GLA_ROWS = 128
GLA_SUB = 8
GLA_STEP_HEADS = 4
FLASH_HEADS = 4
SAMPLE_T = 8
NEG_BIG = -1e30
LOG2_E = 1.4426950408889634


def _pick_block(n, pref, mult=V7X_LANES):
    b = (min(pref, n) // mult) * mult
    while b > mult and n % b:
        b -= mult
    assert b > 0 and n % b == 0, (n, pref, mult)
    return b


def _cparams(sem):
    return pltpu.CompilerParams(dimension_semantics=sem, vmem_limit_bytes=V7X_VMEM_LIMIT_BYTES)


def _dot(a, b):
    return jnp.dot(a, b, preferred_element_type=F32)


def _dot_nt(a, b):
    return lax.dot_general(a, b, (((1,), (1,)), ((), ())), preferred_element_type=F32)


def _norm_lr_kernel(x_ref, g_ref, wlr_ref, wa2_ref, ba_ref, h_ref, la_ref):
    x = x_ref[...]
    ms = jnp.mean(x * x, axis=-1, keepdims=True)
    h = (x * lax.rsqrt(ms + EPS) * g_ref[...]).astype(BF16)
    h_ref[...] = h
    g_lr = _dot_nt(h, wlr_ref[...])
    xa = _dot(g_lr.astype(BF16), wa2_ref[...]) + ba_ref[...]
    la_ref[...] = jax.nn.log_sigmoid(xa) * (1.0 / GLA_TAU)


def _norm_lr(x, g, w_lr, w_a2, b_a):
    m = x.shape[0]
    bm = _pick_block(m, 256)
    return pl.pallas_call(
        _norm_lr_kernel,
        grid=(m // bm,),
        in_specs=[
            pl.BlockSpec((bm, D_MODEL), lambda i: (i, 0)),
            pl.BlockSpec((1, D_MODEL), lambda i: (0, 0)),
            pl.BlockSpec((V7X_LANES, D_MODEL), lambda i: (0, 0)),
            pl.BlockSpec((V7X_LANES, GLA_KEY), lambda i: (0, 0)),
            pl.BlockSpec((1, GLA_KEY), lambda i: (0, 0)),
        ],
        out_specs=[
            pl.BlockSpec((bm, D_MODEL), lambda i: (i, 0)),
            pl.BlockSpec((bm, GLA_KEY), lambda i: (i, 0)),
        ],
        out_shape=[
            jax.ShapeDtypeStruct((m, D_MODEL), BF16),
            jax.ShapeDtypeStruct((m, GLA_KEY), F32),
        ],
        compiler_params=_cparams(("parallel",)),
        name="norm_lr",
    )(x, g, w_lr, w_a2, b_a)


def _in_proj_kernel(a_ref, wt_ref, o_ref):
    o_ref[...] = _dot_nt(a_ref[...], wt_ref[...])


def _in_proj(h, w_t):
    m, k = h.shape
    n = w_t.shape[0]
    bm = _pick_block(m, 1024)
    bn = _pick_block(n, 1024)
    return pl.pallas_call(
        _in_proj_kernel,
        grid=(m // bm, n // bn),
        in_specs=[
            pl.BlockSpec((bm, k), lambda i, j: (i, 0)),
            pl.BlockSpec((bn, k), lambda i, j: (j, 0)),
        ],
        out_specs=pl.BlockSpec((bm, bn), lambda i, j: (i, j)),
        out_shape=jax.ShapeDtypeStruct((m, n), F32),
        compiler_params=_cparams(("parallel", "arbitrary")),
        name="in_proj",
    )(h, w_t)


def _vt_proj_kernel(w_ref, h_ref, o_ref):
    o_ref[...] = _dot_nt(w_ref[...], h_ref[...]).astype(BF16)


def _vt_proj(w_t, h):
    k = w_t.shape[1]
    n = DIFF_WIDTH
    m = h.shape[0]
    bm = _pick_block(m, 1024)
    bn = _pick_block(n, 1024)
    row0 = Z_DV // bn
    return pl.pallas_call(
        _vt_proj_kernel,
        grid=(m // bm, n // bn),
        in_specs=[
            pl.BlockSpec((bn, k), lambda i, j: (row0 + j, 0)),
            pl.BlockSpec((bm, k), lambda i, j: (i, 0)),
        ],
        out_specs=pl.BlockSpec((bn, bm), lambda i, j: (j, i)),
        out_shape=jax.ShapeDtypeStruct((n, m), BF16),
        compiler_params=_cparams(("parallel", "arbitrary")),
        name="vt_proj",
    )(w_t, h)


def _merge_kernel(ya_ref, yb_ref, wa_ref, wb_ref, ga_ref, gb_ref, u_ref):
    pa = _dot(ya_ref[...].astype(BF16), wa_ref[...])
    pb = _dot(yb_ref[...].astype(BF16), wb_ref[...])
    u = jax.nn.sigmoid(ga_ref[...]) * pa + jax.nn.sigmoid(gb_ref[...]) * pb
    u_ref[...] = u.astype(BF16)


def _merge(ya, yb, w_pa, w_pb, z):
    m = ya.shape[0]
    bm = _pick_block(m, 512)
    bn = 1024
    ga0, gb0 = Z_GA // bn, Z_GB // bn
    return pl.pallas_call(
        _merge_kernel,
        grid=(m // bm, D_MODEL // bn),
        in_specs=[
            pl.BlockSpec((bm, GLA_WIDTH), lambda i, j: (i, 0)),
            pl.BlockSpec((bm, DIFF_WIDTH), lambda i, j: (i, 0)),
            pl.BlockSpec((GLA_WIDTH, bn), lambda i, j: (0, j)),
            pl.BlockSpec((DIFF_WIDTH, bn), lambda i, j: (0, j)),
            pl.BlockSpec((bm, bn), lambda i, j: (i, ga0 + j)),
            pl.BlockSpec((bm, bn), lambda i, j: (i, gb0 + j)),
        ],
        out_specs=pl.BlockSpec((bm, bn), lambda i, j: (i, j)),
        out_shape=jax.ShapeDtypeStruct((m, D_MODEL), BF16),
        compiler_params=_cparams(("parallel", "arbitrary")),
        name="merge",
    )(ya, yb, w_pa, w_pb, z, z)


def _oproj_kernel(u_ref, w_ref, x_ref, o_ref):
    o_ref[...] = x_ref[...] + _dot(u_ref[...], w_ref[...])


def _oproj(u, w_o, x):
    m = u.shape[0]
    bm = _pick_block(m, 1024)
    bn = 1024
    return pl.pallas_call(
        _oproj_kernel,
        grid=(m // bm, D_MODEL // bn),
        in_specs=[
            pl.BlockSpec((bm, D_MODEL), lambda i, j: (i, 0)),
            pl.BlockSpec((D_MODEL, bn), lambda i, j: (0, j)),
            pl.BlockSpec((bm, bn), lambda i, j: (i, j)),
        ],
        out_specs=pl.BlockSpec((bm, bn), lambda i, j: (i, j)),
        out_shape=jax.ShapeDtypeStruct((m, D_MODEL), F32),
        compiler_params=_cparams(("parallel", "arbitrary")),
        name="o_proj",
    )(u, w_o, x)


def _rms(x, g):
    return x * lax.rsqrt(jnp.mean(x * x, axis=-1, keepdims=True) + EPS) * g


def _mlp_kernel(x_ref, gm_ref, gf_ref, wu_ref, wd_ref, y_ref, h_s, t0_s, t1_s, *, final_norm):
    f = pl.program_id(1)
    last = pl.num_programs(1) - 1

    def up():
        return jnp.square(jnp.maximum(_dot(h_s[...], wu_ref[...]), 0.0)).astype(BF16)

    @pl.when(f == 0)
    def _():
        h_s[...] = _rms(x_ref[...], gm_ref[...]).astype(BF16)
        t0_s[...] = up()

    @pl.when(f == 1)
    def _():
        y_ref[...] = _dot(t0_s[...], wd_ref[...])
        t1_s[...] = up()

    def middle(t_prev, t_next):
        y_ref[...] += _dot(t_prev[...], wd_ref[...])
        t_next[...] = up()

    @pl.when((f > 1) & (f < last) & (f % 2 == 0))
    def _():
        middle(t1_s, t0_s)

    @pl.when((f > 1) & (f < last) & (f % 2 == 1))
    def _():
        middle(t0_s, t1_s)

    def finish(t_prev):
        x = x_ref[...] + y_ref[...] + _dot(t_prev[...], wd_ref[...])
        y_ref[...] = _rms(x, gf_ref[...]) if final_norm else x

    @pl.when((f == last) & (f % 2 == 0))
    def _():
        finish(t1_s)

    @pl.when((f == last) & (f % 2 == 1))
    def _():
        finish(t0_s)


def _mlp(x2, g_mlp, g_final, w_up, w_down, final_norm):
    m = x2.shape[0]
    d_ff = w_up.shape[1]
    bm = _pick_block(m, 512)
    bf = _pick_block(d_ff, 512)
    nf = d_ff // bf
    assert nf >= 2
    return pl.pallas_call(
        functools.partial(_mlp_kernel, final_norm=final_norm),
        grid=(m // bm, nf + 1),
        in_specs=[
            pl.BlockSpec((bm, D_MODEL), lambda i, f: (i, 0), pipeline_mode=pl.Buffered(1)),
            pl.BlockSpec((1, D_MODEL), lambda i, f: (0, 0)),
            pl.BlockSpec((1, D_MODEL), lambda i, f: (0, 0)),
            pl.BlockSpec((D_MODEL, bf), lambda i, f: (0, jnp.minimum(f, nf - 1))),
            pl.BlockSpec((bf, D_MODEL), lambda i, f: (jnp.maximum(f - 1, 0), 0)),
        ],
        out_specs=pl.BlockSpec((bm, D_MODEL), lambda i, f: (i, 0)),
        out_shape=jax.ShapeDtypeStruct((m, D_MODEL), F32),
        scratch_shapes=[pltpu.VMEM((bm, D_MODEL), BF16),
                        pltpu.VMEM((bm, bf), BF16), pltpu.VMEM((bm, bf), BF16)],
        compiler_params=_cparams(("parallel", "arbitrary")),
        name="mlp",
    )(x2, g_mlp, g_final, w_up, w_down)


def _split3(x):
    hi = x.astype(BF16)
    r1 = x - hi.astype(F32)
    mid = r1.astype(BF16)
    lo = (r1 - mid.astype(F32)).astype(BF16)
    return hi, mid, lo


def _select_sum(sel, x):
    hi, mid, lo = _split3(x)
    return _dot(sel, hi) + _dot(sel, mid) + _dot(sel, lo)


def _group_row_bcast(x, group, row):
    r, c = x.shape
    x3 = x.reshape(r // group, group, c)
    return jnp.broadcast_to(x3[:, row:row + 1, :], x3.shape).reshape(r, c)


def _gla_intra(q, k, b, sub, span):
    r, dk = q.shape
    n = r // sub
    q3 = q.reshape(n, sub, dk)
    k3 = k.reshape(n, sub, dk)
    b3 = b.reshape(n, sub, dk)
    rowi = lax.broadcasted_iota(jnp.int32, (n, sub, dk), 1)
    lane = lax.broadcasted_iota(jnp.int32, (n, sub, dk), 2)
    band0 = lax.broadcasted_iota(jnp.int32, (n, sub, dk), 0) * sub
    a3 = jnp.zeros((n, sub, dk), F32)
    for s in range(sub):
        bs = b3[:, s:s + 1, :]
        ks = k3[:, s:s + 1, :]
        e = jnp.exp(jnp.minimum(b3 - bs, 0.0))
        val = jnp.sum(q3 * e * ks, axis=-1, keepdims=True)
        a3 = jnp.where((lane == band0 + s) & (rowi >= s), val, a3)
    a = a3.reshape(r, dk)

    row = lax.broadcasted_iota(jnp.int32, (r, r), 0)
    col = lax.broadcasted_iota(jnp.int32, (r, r), 1)
    row1 = lax.broadcasted_iota(jnp.int32, (r, 1), 0)
    half = sub
    while half < span:
        group = 2 * half
        ref_b = _group_row_bcast(b, group, half - 1)
        upper = (row1 % group) >= half
        qh = jnp.where(upper, q * jnp.exp(jnp.minimum(b - ref_b, 0.0)), 0.0)
        kh = jnp.where(upper, 0.0, k * jnp.exp(jnp.minimum(ref_b - b, 0.0)))
        a_l = _dot_nt(qh.astype(BF16), kh.astype(BF16))
        a = a + jnp.where((row // group) == (col // group), a_l, 0.0)
        half = group
    return a


def _gla_head_out(o, gn, r):
    o = o * lax.rsqrt(jnp.mean(o * o, axis=-1, keepdims=True) + EPS) * gn
    return o * (r * jax.nn.sigmoid(r))


def _gla_prompt_kernel(q_ref, k_ref, v_ref, r_ref, la_ref, gn_ref, y_ref, sfin_ref, s_ref):
    i = pl.program_id(1)
    rows = q_ref.shape[0]

    @pl.when(i == 0)
    def _():
        s_ref[...] = jnp.zeros_like(s_ref)

    row = lax.broadcasted_iota(jnp.int32, (rows, rows), 0)
    col = lax.broadcasted_iota(jnp.int32, (rows, rows), 1)
    tri = (col <= row).astype(BF16)
    for hh in range(GLA_STEP_HEADS):
        ck = slice(hh * GLA_DK, (hh + 1) * GLA_DK)
        cv = slice(hh * GLA_DV, (hh + 1) * GLA_DV)
        q = q_ref[:, ck] * (GLA_DK ** -0.5)
        k = k_ref[:, ck]
        v = v_ref[:, cv].astype(BF16)
        la = la_ref[:, ck]

        b = _select_sum(tri, la)
        b_last = b[rows - 1:rows, :]

        a = _gla_intra(q, k, b, GLA_SUB, rows)
        s_old = s_ref[hh]
        o = _dot((q * jnp.exp(b)).astype(BF16), s_old.astype(BF16)) + _dot(a.astype(BF16), v)
        y_ref[:, cv] = _gla_head_out(o, gn_ref[:, cv], r_ref[:, cv])

        kd_t = (k * jnp.exp(b_last - b)).T
        decay = jnp.exp(jnp.sum(la.T, axis=1, keepdims=True))
        s_ref[hh] = decay * s_old + _dot(kd_t.astype(BF16), v)

    @pl.when(i == pl.num_programs(1) - 1)
    def _():
        sfin_ref[...] = s_ref[...]


def _gla_prompt(z, la, gn):
    t = z.shape[0]
    rows = GLA_ROWS
    nh = GLA_STEP_HEADS
    qk = GLA_DK * nh
    vv = GLA_DV * nh
    return pl.pallas_call(
        _gla_prompt_kernel,
        grid=(GLA_HEADS // nh, t // rows),
        in_specs=[
            pl.BlockSpec((rows, qk), lambda h, i: (i, Z_GQ // qk + h)),
            pl.BlockSpec((rows, qk), lambda h, i: (i, Z_GK // qk + h)),
            pl.BlockSpec((rows, vv), lambda h, i: (i, Z_GV // vv + h)),
            pl.BlockSpec((rows, vv), lambda h, i: (i, Z_GR // vv + h)),
            pl.BlockSpec((rows, qk), lambda h, i: (i, h)),
            pl.BlockSpec((1, vv), lambda h, i: (0, h)),
        ],
        out_specs=[
            pl.BlockSpec((rows, vv), lambda h, i: (i, h)),
            pl.BlockSpec((nh, GLA_DK, GLA_DV), lambda h, i: (h, 0, 0)),
        ],
        out_shape=[
            jax.ShapeDtypeStruct((t, GLA_WIDTH), F32),
            jax.ShapeDtypeStruct((GLA_HEADS, GLA_DK, GLA_DV), F32),
        ],
        scratch_shapes=[pltpu.VMEM((nh, GLA_DK, GLA_DV), F32)],
        compiler_params=_cparams(("parallel", "arbitrary")),
        name="gla_prompt",
    )(z, z, z, z, la, gn)


def _gla_sample_kernel(q_ref, k_ref, v_ref, r_ref, la_ref, gn_ref, s0_ref, y_ref, sfin_ref):
    rows = q_ref.shape[0]
    n_seq = rows // SAMPLE_T
    q = q_ref[...] * (GLA_DK ** -0.5)
    k = k_ref[...]
    v = v_ref[...].astype(BF16)
    la = la_ref[...]

    row = lax.broadcasted_iota(jnp.int32, (rows, rows), 0)
    col = lax.broadcasted_iota(jnp.int32, (rows, rows), 1)
    same_seq = (row // SAMPLE_T) == (col // SAMPLE_T)
    tri = ((col <= row) & same_seq).astype(BF16)
    b = _select_sum(tri, la)
    b_last = _group_row_bcast(b, SAMPLE_T, SAMPLE_T - 1)

    a = _gla_intra(q, k, b, SAMPLE_T, SAMPLE_T)
    o = _dot(a.astype(BF16), v)

    qd = q * jnp.exp(b)
    kd_t = (k * jnp.exp(b_last - b)).T
    la_t = la.T
    row1 = lax.broadcasted_iota(jnp.int32, (rows, 1), 0)
    lane1 = lax.broadcasted_iota(jnp.int32, (1, rows), 1)
    for sq in range(n_seq):
        s0 = s0_ref[sq, 0]
        in_rows = (row1 // SAMPLE_T) == sq
        in_lanes = (lane1 // SAMPLE_T) == sq
        o = o + _dot(jnp.where(in_rows, qd, 0.0).astype(BF16), s0.astype(BF16))
        decay = jnp.exp(jnp.sum(jnp.where(in_lanes, la_t, 0.0), axis=1, keepdims=True))
        sfin_ref[sq, 0] = decay * s0 + _dot(jnp.where(in_lanes, kd_t, 0.0).astype(BF16), v)
    y_ref[...] = _gla_head_out(o, gn_ref[...], r_ref[...])


def _gla_sample(z, la, gn, state):
    m = z.shape[0]
    rows = GLA_ROWS
    n_seq = rows // SAMPLE_T
    qk = GLA_DK
    vv = GLA_DV
    return pl.pallas_call(
        _gla_sample_kernel,
        grid=(GLA_HEADS, m // rows),
        in_specs=[
            pl.BlockSpec((rows, qk), lambda h, i: (i, Z_GQ // qk + h)),
            pl.BlockSpec((rows, qk), lambda h, i: (i, Z_GK // qk + h)),
            pl.BlockSpec((rows, vv), lambda h, i: (i, Z_GV // vv + h)),
            pl.BlockSpec((rows, vv), lambda h, i: (i, Z_GR // vv + h)),
            pl.BlockSpec((rows, qk), lambda h, i: (i, h)),
            pl.BlockSpec((1, vv), lambda h, i: (0, h)),
            pl.BlockSpec((n_seq, 1, qk, vv), lambda h, i: (i, h, 0, 0)),
        ],
        out_specs=[
            pl.BlockSpec((rows, vv), lambda h, i: (i, h)),
            pl.BlockSpec((n_seq, 1, qk, vv), lambda h, i: (i, h, 0, 0)),
        ],
        out_shape=[
            jax.ShapeDtypeStruct((m, GLA_WIDTH), F32),
            jax.ShapeDtypeStruct(state.shape, F32),
        ],
        compiler_params=_cparams(("parallel", "parallel")),
        name="gla_sample",
    )(z, z, z, z, la, gn, state)


def _lambda_value(lam_ref, lam_init):
    l = lam_ref[...]
    s1 = jnp.sum(l[0:1] * l[1:2], axis=-1, keepdims=True)
    s2 = jnp.sum(l[2:3] * l[3:4], axis=-1, keepdims=True)
    return jnp.exp(s1) - jnp.exp(s2) + lam_init


def _diff_head_out(acc1, l1, acc2, l2, lam, gn, lam_init):
    o = acc1 / l1 - lam * (acc2 / l2)
    o = o * lax.rsqrt(jnp.mean(o * o, axis=-1, keepdims=True) + EPS)
    return o * gn * (1.0 - lam_init)


def _diff_prompt_kernel(qi_ref, ki_ref, lam_ref, q_ref, k_ref, vt_ref, gn_ref, y_ref,
                        q_s, m_s, l_s, acc_s, *, lam_init):
    hg = pl.program_id(0)
    step_id = pl.program_id(1)
    qi = qi_ref[step_id]
    ki = ki_ref[step_id]
    bq = q_ref.shape[0]
    bk = k_ref.shape[0]
    w = 2 * DIFF_DK

    @pl.when(ki == 0)
    def _():
        q_s[...] = (q_ref[...] * (DIFF_DK ** -0.5 * LOG2_E)).astype(BF16)
        m_s[...] = jnp.full_like(m_s, NEG_BIG)
        l_s[...] = jnp.zeros_like(l_s)
        acc_s[...] = jnp.zeros_like(acc_s)

    def step(masked):
        kpos = (ki * bk - qi * bq + lax.broadcasted_iota(jnp.int32, (bk, V7X_LANES), 0)).astype(F32)
        if masked:
            row = lax.broadcasted_iota(jnp.int32, (bk, bq), 0)
            col = lax.broadcasted_iota(jnp.int32, (bk, bq), 1)
            keep = row <= col
        for hh in range(FLASH_HEADS):
            k = k_ref[:, hh * w:(hh + 1) * w].astype(BF16)
            vt = vt_ref[hh * DIFF_DV:(hh + 1) * DIFF_DV, :]
            head = jnp.full((bk, V7X_LANES), hg * FLASH_HEADS + hh + 1, jnp.int32).astype(F32)
            slope = jnp.exp2(-head * (8.0 / DIFF_HEADS)) * LOG2_E
            bias = jnp.tile(slope * kpos, (1, bq // V7X_LANES))
            for mp in range(2):
                c = hh * w + mp * DIFF_DK
                idx = hh * 2 + mp
                s = _dot_nt(k[:, mp * DIFF_DK:(mp + 1) * DIFF_DK], q_s[:, c:c + DIFF_DK]) + bias
                if masked:
                    s = jnp.where(keep, s, -jnp.inf)
                m_old = m_s[idx]
                m_new = jnp.maximum(m_old, jnp.max(s, axis=0, keepdims=True))
                alpha = jnp.exp2(m_old - m_new)
                p = jnp.exp2(s - m_new)
                l_s[idx] = alpha * l_s[idx] + jnp.sum(p, axis=0, keepdims=True)
                acc_s[idx] = alpha * acc_s[idx] + _dot(vt, p.astype(BF16))
                m_s[idx] = m_new

    @pl.when(ki < qi)
    def _():
        step(False)

    @pl.when(ki == qi)
    def _():
        step(True)
        lam = _lambda_value(lam_ref, lam_init)
        for hh in range(FLASH_HEADS):
            i1, i2 = hh * 2, hh * 2 + 1
            o = acc_s[i1] / l_s[i1] - lam * (acc_s[i2] / l_s[i2])
            o = o * lax.rsqrt(jnp.mean(o * o, axis=0, keepdims=True) + EPS)
            gn = gn_ref[hh * DIFF_DV:(hh + 1) * DIFF_DV, :]
            y_ref[:, hh * DIFF_DV:(hh + 1) * DIFF_DV] = (o * gn * (1.0 - lam_init)).T


def _diff_prompt(z, vt, lam_vecs, gn_col, lam_init):
    t = z.shape[0]
    bq = _pick_block(t, 512)
    w = 2 * DIFF_DK * FLASH_HEADS
    dvw = DIFF_DV * FLASH_HEADS
    nq = t // bq
    pairs = [(qi, ki) for qi in range(nq) for ki in range(qi + 1)]
    qi_arr = jnp.asarray([p[0] for p in pairs], jnp.int32)
    ki_arr = jnp.asarray([p[1] for p in pairs], jnp.int32)
    grid_spec = pltpu.PrefetchScalarGridSpec(
        num_scalar_prefetch=2,
        grid=(DIFF_HEADS // FLASH_HEADS, len(pairs)),
        in_specs=[
            pl.BlockSpec((4, DIFF_DK), lambda h, s, qi, ki: (0, 0)),
            pl.BlockSpec((bq, w), lambda h, s, qi, ki: (qi[s], Z_DQ // w + h)),
            pl.BlockSpec((bq, w), lambda h, s, qi, ki: (ki[s], Z_DK // w + h)),
            pl.BlockSpec((dvw, bq), lambda h, s, qi, ki: (h, ki[s])),
            pl.BlockSpec((dvw, 1), lambda h, s, qi, ki: (h, 0)),
        ],
        out_specs=pl.BlockSpec((bq, dvw), lambda h, s, qi, ki: (qi[s], h)),
        scratch_shapes=[
            pltpu.VMEM((bq, w), BF16),
            pltpu.VMEM((2 * FLASH_HEADS, 1, bq), F32),
            pltpu.VMEM((2 * FLASH_HEADS, 1, bq), F32),
            pltpu.VMEM((2 * FLASH_HEADS, DIFF_DV, bq), F32),
        ],
    )
    return pl.pallas_call(
        functools.partial(_diff_prompt_kernel, lam_init=lam_init),
        grid_spec=grid_spec,
        out_shape=jax.ShapeDtypeStruct((t, DIFF_WIDTH), F32),
        compiler_params=_cparams(("parallel", "arbitrary")),
        name="diff_prompt",
    )(qi_arr, ki_arr, lam_vecs, z, z, vt, gn_col)


def _diff_sample_kernel(pt_ref, lam_ref, q_ref, kn_ref, vn_ref, *rest, lam_init, past_len, pps):
    del pt_ref
    kp_refs, vp_refs = rest[:pps], rest[pps:2 * pps]
    gn_ref, y_ref, q_s, bias_s, m_s, l_s, acc_s = rest[2 * pps:]
    p = pl.program_id(1)
    tn = q_ref.shape[0]
    page = kp_refs[0].shape[2]
    hq = DIFF_HEADS * tn
    n_rows = 2 * hq
    n_keys = page * DIFF_HEADS
    w = 2 * DIFF_DK

    row1 = lax.broadcasted_iota(jnp.int32, (n_rows, 1), 0)
    slope = jnp.exp2(-((row1 % hq) // tn + 1).astype(F32) * (8.0 / DIFF_HEADS))

    def attend(kflat, vflat, bias):
        s = jnp.concatenate(
            [_dot_nt(q_s[mp], kflat[:, mp * DIFF_DK:(mp + 1) * DIFF_DK]) for mp in range(2)],
            axis=0) + bias
        m_old = m_s[...]
        m_new = jnp.maximum(m_old, jnp.max(s, axis=-1, keepdims=True))
        alpha = jnp.exp(m_old - m_new)
        pr = jnp.exp(s - m_new)
        l_s[...] = alpha * l_s[...] + jnp.sum(pr, axis=-1, keepdims=True)
        m_s[...] = m_new
        acc_s[...] = alpha * acc_s[...] + _dot(pr.astype(BF16), vflat)

    @pl.when((pl.program_id(0) == 0) & (p == 0))
    def _():
        row = lax.broadcasted_iota(jnp.int32, (n_rows, n_keys), 0)
        col = lax.broadcasted_iota(jnp.int32, (n_rows, n_keys), 1)
        same_head = (col % DIFF_HEADS) == ((row % hq) // tn)
        bias_s[...] = jnp.where(same_head, slope * (col // DIFF_HEADS).astype(F32), -jnp.inf)

    @pl.when(p == 0)
    def _():
        q = q_ref[...] * (DIFF_DK ** -0.5)
        for mp in range(2):
            q_s[mp] = jnp.concatenate(
                [q[:, hh * w + mp * DIFF_DK:hh * w + (mp + 1) * DIFF_DK] for hh in range(DIFF_HEADS)],
                axis=0).astype(BF16)
        m_s[...] = jnp.full_like(m_s, NEG_BIG)
        l_s[...] = jnp.zeros_like(l_s)
        acc_s[...] = jnp.zeros_like(acc_s)

        pad = jnp.zeros((V7X_LANES - hq, w), F32)
        k_new = jnp.concatenate(
            [kn_ref[:, hh * w:(hh + 1) * w] for hh in range(DIFF_HEADS)] + [pad], axis=0)
        v_new = jnp.concatenate(
            [vn_ref[:, hh * DIFF_DV:(hh + 1) * DIFF_DV] for hh in range(DIFF_HEADS)] + [pad], axis=0)
        rown = lax.broadcasted_iota(jnp.int32, (n_rows, V7X_LANES), 0)
        coln = lax.broadcasted_iota(jnp.int32, (n_rows, V7X_LANES), 1)
        visible = ((coln // tn) == ((rown % hq) // tn)) & ((coln % tn) <= (rown % tn))
        bias_new = jnp.where(visible, slope * (coln % tn).astype(F32), -jnp.inf)
        attend(k_new.astype(BF16), v_new.astype(BF16), bias_new)

    for j in range(pps):
        base = ((p * pps + j) * page - past_len).astype(F32)
        attend(kp_refs[j][0, 0].reshape(n_keys, w).astype(BF16),
               vp_refs[j][0, 0].reshape(n_keys, DIFF_DV).astype(BF16),
               bias_s[...] + slope * base)

    @pl.when(p == pl.num_programs(1) - 1)
    def _():
        lam = _lambda_value(lam_ref, lam_init)
        acc = acc_s[...]
        l = l_s[...]
        for hh in range(DIFF_HEADS):
            r1 = hh * tn
            r2 = hq + hh * tn
            y_ref[:, hh * DIFF_DV:(hh + 1) * DIFF_DV] = _diff_head_out(
                acc[r1:r1 + tn], l[r1:r1 + tn], acc[r2:r2 + tn], l[r2:r2 + tn],
                lam, gn_ref[:, hh * DIFF_DV:(hh + 1) * DIFF_DV], lam_init)


def _diff_sample(z, cache_k, cache_v, layer, page_table, lam_vecs, gn, lam_init):
    m = z.shape[0]
    tn = SAMPLE_T
    n_batch = m // tn
    n_pages = page_table.shape[1]
    page = cache_k.shape[2]
    pps = next(c for c in (8, 4, 2, 1) if n_pages % c == 0)
    pt = page_table.reshape(-1)
    n_rows = DIFF_HEADS * 2 * tn
    w = 2 * DIFF_DK

    def page_spec(width, j):
        return pl.BlockSpec(
            (1, 1, page, DIFF_HEADS, width),
            lambda b, p, pt: (layer, pt[b * n_pages + p * pps + j], 0, 0, 0))

    grid_spec = pltpu.PrefetchScalarGridSpec(
        num_scalar_prefetch=1,
        grid=(n_batch, n_pages // pps),
        in_specs=[
            pl.BlockSpec((4, DIFF_DK), lambda b, p, pt: (0, 0)),
            pl.BlockSpec((tn, DIFF_QK), lambda b, p, pt: (b, Z_DQ // DIFF_QK)),
            pl.BlockSpec((tn, DIFF_QK), lambda b, p, pt: (b, Z_DK // DIFF_QK)),
            pl.BlockSpec((tn, DIFF_WIDTH), lambda b, p, pt: (b, Z_DV // DIFF_WIDTH)),
        ] + [page_spec(w, j) for j in range(pps)] + [page_spec(DIFF_DV, j) for j in range(pps)] + [
            pl.BlockSpec((1, DIFF_WIDTH), lambda b, p, pt: (0, 0)),
        ],
        out_specs=pl.BlockSpec((tn, DIFF_WIDTH), lambda b, p, pt: (b, 0)),
        scratch_shapes=[
            pltpu.VMEM((2, DIFF_HEADS * tn, DIFF_DK), BF16),
            pltpu.VMEM((n_rows, page * DIFF_HEADS), F32),
            pltpu.VMEM((n_rows, 1), F32),
            pltpu.VMEM((n_rows, 1), F32),
            pltpu.VMEM((n_rows, DIFF_DV), F32),
        ],
    )
    return pl.pallas_call(
        functools.partial(_diff_sample_kernel, lam_init=lam_init, past_len=n_pages * page, pps=pps),
        grid_spec=grid_spec,
        out_shape=jax.ShapeDtypeStruct((m, DIFF_WIDTH), F32),
        compiler_params=_cparams(("arbitrary", "arbitrary")),
        name="diff_sample",
    )(pt, lam_vecs, z, z, z, *([cache_k] * pps), *([cache_v] * pps), gn)


def _prep_weights(w_in, w_alpha2, b_alpha, w_proj_a, w_proj_b, w_o, w_up, w_down):
    w_t = w_in.T
    w_main_t = jnp.concatenate(
        [w_t[:LR_START].astype(BF16), w_t[LR_START + GLA_RANK:].astype(BF16)], axis=0)
    w_lr_t = jnp.pad(w_t[LR_START:LR_START + GLA_RANK],
                     ((0, V7X_LANES - GLA_RANK), (0, 0))).astype(BF16)
    w_a2 = jnp.pad(w_alpha2, ((0, V7X_LANES - GLA_RANK), (0, 0))).astype(BF16)
    return dict(w_main_t=w_main_t, w_lr_t=w_lr_t, w_a2=w_a2, b_a=b_alpha.reshape(1, GLA_KEY),
                w_pa=w_proj_a.astype(BF16), w_pb=w_proj_b.astype(BF16), w_o=w_o.astype(BF16),
                w_up=w_up.astype(BF16), w_down=w_down.astype(BF16))


def _mix_inputs(x, norm_g, wts):
    h, la = _norm_lr(x, norm_g.reshape(1, D_MODEL), wts["w_lr_t"], wts["w_a2"], wts["b_a"])
    z = _in_proj(h, wts["w_main_t"])
    return z, la, h


def _channel_mix(x, ya, yb, z, wts, norm_mlp, fin, last):
    u = _merge(ya, yb, wts["w_pa"], wts["w_pb"], z)
    x2 = _oproj(u, wts["w_o"], x)
    return _mlp(x2, norm_mlp.reshape(1, D_MODEL), fin, wts["w_up"], wts["w_down"], last)


def kernel(x_prompt, x_sample, cache_k, cache_v, state_gla, page_table, norm_mix, w_in, w_alpha2,
           b_alpha, gla_norm, lambda_q1, lambda_k1, lambda_q2, lambda_k2, diff_norm, w_proj_a,
           w_proj_b, w_o, norm_mlp, w_up, w_down, final_norm):
    depth = w_in.shape[0]
    bp, tp = x_prompt.shape[0], x_prompt.shape[1]
    bs, ts = x_sample.shape[0], x_sample.shape[1]
    assert bp == 1 and ts == SAMPLE_T
    xp = x_prompt.reshape(tp, D_MODEL)
    xs = x_sample.reshape(bs * ts, D_MODEL)
    fin = final_norm.reshape(1, D_MODEL)
    kp_l, vp_l, sp_l, ks_l, vs_l, ss_l = [], [], [], [], [], []
    for l in range(depth):
        lam_init = 0.8 - 0.6 * math.exp(-0.3 * l)
        wts = _prep_weights(w_in[l], w_alpha2[l], b_alpha[l], w_proj_a[l], w_proj_b[l], w_o[l],
                            w_up[l], w_down[l])
        lam_vecs = jnp.stack([lambda_q1[l], lambda_k1[l], lambda_q2[l], lambda_k2[l]]).astype(F32)
        gn_a = gla_norm[l].reshape(1, GLA_WIDTH)
        gn_b = diff_norm[l].reshape(1, DIFF_WIDTH)
        last = l == depth - 1

        z, la, h = _mix_inputs(xp, norm_mix[l], wts)
        ya, s_fin = _gla_prompt(z, la, gn_a)
        vt = _vt_proj(wts["w_main_t"], h)
        yb = _diff_prompt(z, vt, lam_vecs, diff_norm[l].reshape(DIFF_WIDTH, 1), lam_init)
        xp = _channel_mix(xp, ya, yb, z, wts, norm_mlp[l], fin, last)
        kp_l.append(z[:, Z_DK:Z_DK + DIFF_QK].reshape(bp, tp, DIFF_HEADS, 2 * DIFF_DK))
        vp_l.append(z[:, Z_DV:Z_DV + DIFF_WIDTH].reshape(bp, tp, DIFF_HEADS, DIFF_DV))
        sp_l.append(s_fin.reshape(bp, GLA_HEADS, GLA_DK, GLA_DV))

        z, la, _ = _mix_inputs(xs, norm_mix[l], wts)
        ya, s_fin = _gla_sample(z, la, gn_a, state_gla[l])
        yb = _diff_sample(z, cache_k, cache_v, l, page_table, lam_vecs, gn_b, lam_init)
        xs = _channel_mix(xs, ya, yb, z, wts, norm_mlp[l], fin, last)
        ks_l.append(z[:, Z_DK:Z_DK + DIFF_QK].reshape(bs, ts, DIFF_HEADS, 2 * DIFF_DK))
        vs_l.append(z[:, Z_DV:Z_DV + DIFF_WIDTH].reshape(bs, ts, DIFF_HEADS, DIFF_DV))
        ss_l.append(s_fin)

    return (xp.reshape(bp, tp, D_MODEL), xs.reshape(bs, ts, D_MODEL), jnp.stack(kp_l),
            jnp.stack(vp_l), jnp.stack(sp_l), jnp.stack(ks_l), jnp.stack(vs_l), jnp.stack(ss_l))
```

```python
import functools
import math

import jax
import jax.numpy as jnp
from jax import lax
from jax.experimental import pallas as pl
from jax.experimental.pallas import tpu as pltpu

F32 = jnp.float32
BF16 = jnp.bfloat16

D_MODEL = 4096
PAGE_SIZE = 128
GLA_HEADS = 8
GLA_DK = 128
GLA_DV = 256
GLA_KEY = GLA_HEADS * GLA_DK
GLA_WIDTH = GLA_HEADS * GLA_DV
GLA_RANK = 16
GLA_TAU = 16.0
DIFF_HEADS = 8
DIFF_DK = 128
DIFF_DV = 256
DIFF_QK = DIFF_HEADS * 2 * DIFF_DK
DIFF_WIDTH = DIFF_HEADS * DIFF_DV
EPS = 1e-6

Z_GQ = 0
Z_GK = Z_GQ + GLA_KEY
Z_GV = Z_GK + GLA_KEY
Z_GR = Z_GV + GLA_WIDTH
Z_DQ = Z_GR + GLA_WIDTH
Z_DK = Z_DQ + DIFF_QK
Z_DV = Z_DK + DIFF_QK
Z_GA = Z_DV + DIFF_WIDTH
Z_GB = Z_GA + D_MODEL
Z_COLS = Z_GB + D_MODEL
LR_START = 2 * GLA_KEY + 2 * GLA_WIDTH

V7X_LANES = 128
V7X_SUBLANES = 8
V7X_VMEM_LIMIT_BYTES = 56 * 1024 * 1024

GLA_ROWS = 128
GLA_SUB = 8
GLA_STEP_HEADS = 8
FLASH_HEADS = 8
SAMPLE_T = 8
NEG_BIG = -1e30
LOG2_E = 1.4426950408889634


def _pick_block(n, pref, mult=V7X_LANES):
    b = (min(pref, n) // mult) * mult
    while b > mult and n % b:
        b -= mult
    assert b > 0 and n % b == 0, (n, pref, mult)
    return b


def _cparams(sem):
    return pltpu.CompilerParams(dimension_semantics=sem, vmem_limit_bytes=V7X_VMEM_LIMIT_BYTES)


def _dot(a, b):
    return jnp.dot(a, b, preferred_element_type=F32)


def _dot_nt(a, b):
    return lax.dot_general(a, b, (((1,), (1,)), ((), ())), preferred_element_type=F32)


def _norm_lr_kernel(x_ref, g_ref, wlr_ref, wa2_ref, ba_ref, h_ref, la_ref):
    x = x_ref[...]
    ms = jnp.mean(x * x, axis=-1, keepdims=True)
    h = (x * lax.rsqrt(ms + EPS) * g_ref[...]).astype(BF16)
    h_ref[...] = h
    g_lr = _dot_nt(h, wlr_ref[...])
    xa = _dot(g_lr.astype(BF16), wa2_ref[...]) + ba_ref[...]
    la_ref[...] = jax.nn.log_sigmoid(xa) * (1.0 / GLA_TAU)


def _norm_lr(x, g, w_lr, w_a2, b_a):
    m = x.shape[0]
    bm = _pick_block(m, 256)
    return pl.pallas_call(
        _norm_lr_kernel,
        grid=(m // bm,),
        in_specs=[
            pl.BlockSpec((bm, D_MODEL), lambda i: (i, 0)),
            pl.BlockSpec((1, D_MODEL), lambda i: (0, 0)),
            pl.BlockSpec((V7X_LANES, D_MODEL), lambda i: (0, 0)),
            pl.BlockSpec((V7X_LANES, GLA_KEY), lambda i: (0, 0)),
            pl.BlockSpec((1, GLA_KEY), lambda i: (0, 0)),
        ],
        out_specs=[
            pl.BlockSpec((bm, D_MODEL), lambda i: (i, 0)),
            pl.BlockSpec((bm, GLA_KEY), lambda i: (i, 0)),
        ],
        out_shape=[
            jax.ShapeDtypeStruct((m, D_MODEL), BF16),
            jax.ShapeDtypeStruct((m, GLA_KEY), F32),
        ],
        compiler_params=_cparams(("parallel",)),
        name="norm_lr",
    )(x, g, w_lr, w_a2, b_a)


def _in_proj_kernel(a_ref, wt_ref, o_ref):
    o_ref[...] = _dot_nt(a_ref[...], wt_ref[...])


def _in_proj(h, w_t):
    m, k = h.shape
    n = w_t.shape[0]
    bm = _pick_block(m, 1024)
    bn = _pick_block(n, 1024)
    return pl.pallas_call(
        _in_proj_kernel,
        grid=(m // bm, n // bn),
        in_specs=[
            pl.BlockSpec((bm, k), lambda i, j: (i, 0)),
            pl.BlockSpec((bn, k), lambda i, j: (j, 0)),
        ],
        out_specs=pl.BlockSpec((bm, bn), lambda i, j: (i, j)),
        out_shape=jax.ShapeDtypeStruct((m, n), F32),
        compiler_params=_cparams(("parallel", "arbitrary")),
        name="in_proj",
    )(h, w_t)


def _vt_proj_kernel(w_ref, h_ref, o_ref):
    o_ref[...] = _dot_nt(w_ref[...], h_ref[...]).astype(BF16)


def _vt_proj(w_t, h):
    k = w_t.shape[1]
    n = DIFF_WIDTH
    m = h.shape[0]
    bm = _pick_block(m, 1024)
    bn = _pick_block(n, 1024)
    row0 = Z_DV // bn
    return pl.pallas_call(
        _vt_proj_kernel,
        grid=(m // bm, n // bn),
        in_specs=[
            pl.BlockSpec((bn, k), lambda i, j: (row0 + j, 0)),
            pl.BlockSpec((bm, k), lambda i, j: (i, 0)),
        ],
        out_specs=pl.BlockSpec((bn, bm), lambda i, j: (j, i)),
        out_shape=jax.ShapeDtypeStruct((n, m), BF16),
        compiler_params=_cparams(("parallel", "arbitrary")),
        name="vt_proj",
    )(w_t, h)


def _merge_kernel(ya_ref, yb_ref, wa_ref, wb_ref, ga_ref, gb_ref, u_ref, ya_s, yb_s):
    @pl.when(pl.program_id(1) == 0)
    def _():
        ya_s[...] = ya_ref[...].astype(BF16)
        yb_s[...] = yb_ref[...].astype(BF16)

    pa = _dot(ya_s[...], wa_ref[...])
    pb = _dot(yb_s[...], wb_ref[...])
    u = jax.nn.sigmoid(ga_ref[...]) * pa + jax.nn.sigmoid(gb_ref[...]) * pb
    u_ref[...] = u.astype(BF16)


def _merge(ya, yb, w_pa, w_pb, z):
    m = ya.shape[0]
    bm = _pick_block(m, 1024)
    bn = 512
    ga0, gb0 = Z_GA // bn, Z_GB // bn
    return pl.pallas_call(
        _merge_kernel,
        grid=(m // bm, D_MODEL // bn),
        in_specs=[
            pl.BlockSpec((bm, GLA_WIDTH), lambda i, j: (i, 0), pipeline_mode=pl.Buffered(1)),
            pl.BlockSpec((bm, DIFF_WIDTH), lambda i, j: (i, 0), pipeline_mode=pl.Buffered(1)),
            pl.BlockSpec((GLA_WIDTH, bn), lambda i, j: (0, j)),
            pl.BlockSpec((DIFF_WIDTH, bn), lambda i, j: (0, j)),
            pl.BlockSpec((bm, bn), lambda i, j: (i, ga0 + j)),
            pl.BlockSpec((bm, bn), lambda i, j: (i, gb0 + j)),
        ],
        out_specs=pl.BlockSpec((bm, bn), lambda i, j: (i, j)),
        out_shape=jax.ShapeDtypeStruct((m, D_MODEL), BF16),
        scratch_shapes=[pltpu.VMEM((bm, GLA_WIDTH), BF16), pltpu.VMEM((bm, DIFF_WIDTH), BF16)],
        compiler_params=_cparams(("parallel", "arbitrary")),
        name="merge",
    )(ya, yb, w_pa, w_pb, z, z)


def _oproj_kernel(u_ref, w_ref, x_ref, o_ref):
    o_ref[...] = x_ref[...] + _dot(u_ref[...], w_ref[...])


def _oproj(u, w_o, x):
    m = u.shape[0]
    bm = _pick_block(m, 1024)
    bn = 1024
    return pl.pallas_call(
        _oproj_kernel,
        grid=(m // bm, D_MODEL // bn),
        in_specs=[
            pl.BlockSpec((bm, D_MODEL), lambda i, j: (i, 0)),
            pl.BlockSpec((D_MODEL, bn), lambda i, j: (0, j)),
            pl.BlockSpec((bm, bn), lambda i, j: (i, j)),
        ],
        out_specs=pl.BlockSpec((bm, bn), lambda i, j: (i, j)),
        out_shape=jax.ShapeDtypeStruct((m, D_MODEL), F32),
        compiler_params=_cparams(("parallel", "arbitrary")),
        name="o_proj",
    )(u, w_o, x)


def _rms(x, g):
    return x * lax.rsqrt(jnp.mean(x * x, axis=-1, keepdims=True) + EPS) * g


def _mlp_kernel(x_ref, gm_ref, gf_ref, wu_ref, wd_ref, y_ref, h_s, t0_s, t1_s, *, final_norm):
    f = pl.program_id(1)
    last = pl.num_programs(1) - 1

    def up():
        return jnp.square(jnp.maximum(_dot(h_s[...], wu_ref[...]), 0.0)).astype(BF16)

    @pl.when(f == 0)
    def _():
        h_s[...] = _rms(x_ref[...], gm_ref[...]).astype(BF16)
        t0_s[...] = up()

    @pl.when(f == 1)
    def _():
        y_ref[...] = _dot(t0_s[...], wd_ref[...])
        t1_s[...] = up()

    def middle(t_prev, t_next):
        y_ref[...] += _dot(t_prev[...], wd_ref[...])
        t_next[...] = up()

    @pl.when((f > 1) & (f < last) & (f % 2 == 0))
    def _():
        middle(t1_s, t0_s)

    @pl.when((f > 1) & (f < last) & (f % 2 == 1))
    def _():
        middle(t0_s, t1_s)

    def finish(t_prev):
        x = x_ref[...] + y_ref[...] + _dot(t_prev[...], wd_ref[...])
        y_ref[...] = _rms(x, gf_ref[...]) if final_norm else x

    @pl.when((f == last) & (f % 2 == 0))
    def _():
        finish(t1_s)

    @pl.when((f == last) & (f % 2 == 1))
    def _():
        finish(t0_s)


def _mlp(x2, g_mlp, g_final, w_up, w_down, final_norm):
    m = x2.shape[0]
    d_ff = w_up.shape[1]
    bm = _pick_block(m, 512)
    bf = _pick_block(d_ff, 512)
    nf = d_ff // bf
    assert nf >= 2
    return pl.pallas_call(
        functools.partial(_mlp_kernel, final_norm=final_norm),
        grid=(m // bm, nf + 1),
        in_specs=[
            pl.BlockSpec((bm, D_MODEL), lambda i, f: (i, 0), pipeline_mode=pl.Buffered(1)),
            pl.BlockSpec((1, D_MODEL), lambda i, f: (0, 0)),
            pl.BlockSpec((1, D_MODEL), lambda i, f: (0, 0)),
            pl.BlockSpec((D_MODEL, bf), lambda i, f: (0, jnp.minimum(f, nf - 1))),
            pl.BlockSpec((bf, D_MODEL), lambda i, f: (jnp.maximum(f - 1, 0), 0)),
        ],
        out_specs=pl.BlockSpec((bm, D_MODEL), lambda i, f: (i, 0)),
        out_shape=jax.ShapeDtypeStruct((m, D_MODEL), F32),
        scratch_shapes=[pltpu.VMEM((bm, D_MODEL), BF16),
                        pltpu.VMEM((bm, bf), BF16), pltpu.VMEM((bm, bf), BF16)],
        compiler_params=_cparams(("parallel", "arbitrary")),
        name="mlp",
    )(x2, g_mlp, g_final, w_up, w_down)


def _split3(x):
    hi = x.astype(BF16)
    r1 = x - hi.astype(F32)
    mid = r1.astype(BF16)
    lo = (r1 - mid.astype(F32)).astype(BF16)
    return hi, mid, lo


def _select_sum(sel, x):
    hi, mid, lo = _split3(x)
    return _dot(sel, hi) + _dot(sel, mid) + _dot(sel, lo)


def _group_row_bcast(x, group, row):
    r, c = x.shape
    x3 = x.reshape(r // group, group, c)
    return jnp.broadcast_to(x3[:, row:row + 1, :], x3.shape).reshape(r, c)


def _gla_intra(q, k, b, sub, span):
    r, dk = q.shape
    n = r // sub
    q3 = q.reshape(n, sub, dk)
    k3 = k.reshape(n, sub, dk)
    b3 = b.reshape(n, sub, dk)
    rowi = lax.broadcasted_iota(jnp.int32, (n, sub, dk), 1)
    lane = lax.broadcasted_iota(jnp.int32, (n, sub, dk), 2)
    band0 = lax.broadcasted_iota(jnp.int32, (n, sub, dk), 0) * sub
    a3 = jnp.zeros((n, sub, dk), F32)
    for s in range(sub):
        bs = b3[:, s:s + 1, :]
        ks = k3[:, s:s + 1, :]
        e = jnp.exp(jnp.minimum(b3 - bs, 0.0))
        val = jnp.sum(q3 * e * ks, axis=-1, keepdims=True)
        a3 = jnp.where((lane == band0 + s) & (rowi >= s), val, a3)
    a = a3.reshape(r, dk)

    row = lax.broadcasted_iota(jnp.int32, (r, r), 0)
    col = lax.broadcasted_iota(jnp.int32, (r, r), 1)
    row1 = lax.broadcasted_iota(jnp.int32, (r, 1), 0)
    half = sub
    while half < span:
        group = 2 * half
        ref_b = _group_row_bcast(b, group, half - 1)
        upper = (row1 % group) >= half
        qh = jnp.where(upper, q * jnp.exp(jnp.minimum(b - ref_b, 0.0)), 0.0)
        kh = jnp.where(upper, 0.0, k * jnp.exp(jnp.minimum(ref_b - b, 0.0)))
        a_l = _dot_nt(qh.astype(BF16), kh.astype(BF16))
        a = a + jnp.where((row // group) == (col // group), a_l, 0.0)
        half = group
    return a


def _gla_head_out(o, gn, r):
    o = o * lax.rsqrt(jnp.mean(o * o, axis=-1, keepdims=True) + EPS) * gn
    return o * (r * jax.nn.sigmoid(r))


def _gla_prompt_kernel(q_ref, k_ref, v_ref, r_ref, la_ref, gn_ref, y_ref, sfin_ref, s_ref):
    i = pl.program_id(1)
    rows = q_ref.shape[0]

    @pl.when(i == 0)
    def _():
        s_ref[...] = jnp.zeros_like(s_ref)

    row = lax.broadcasted_iota(jnp.int32, (rows, rows), 0)
    col = lax.broadcasted_iota(jnp.int32, (rows, rows), 1)
    tri = (col <= row).astype(BF16)
    for hh in range(GLA_STEP_HEADS):
        ck = slice(hh * GLA_DK, (hh + 1) * GLA_DK)
        cv = slice(hh * GLA_DV, (hh + 1) * GLA_DV)
        q = q_ref[:, ck] * (GLA_DK ** -0.5)
        k = k_ref[:, ck]
        v = v_ref[:, cv].astype(BF16)
        la = la_ref[:, ck]

        b = _select_sum(tri, la)
        b_last = b[rows - 1:rows, :]

        a = _gla_intra(q, k, b, GLA_SUB, rows)
        s_old = s_ref[hh]
        o = _dot((q * jnp.exp(b)).astype(BF16), s_old.astype(BF16)) + _dot(a.astype(BF16), v)
        y_ref[:, cv] = _gla_head_out(o, gn_ref[:, cv], r_ref[:, cv])

        kd_t = (k * jnp.exp(b_last - b)).T
        decay = jnp.exp(jnp.sum(la.T, axis=1, keepdims=True))
        s_ref[hh] = decay * s_old + _dot(kd_t.astype(BF16), v)

    @pl.when(i == pl.num_programs(1) - 1)
    def _():
        sfin_ref[...] = s_ref[...]


def _gla_prompt(z, la, gn):
    t = z.shape[0]
    rows = GLA_ROWS
    nh = GLA_STEP_HEADS
    qk = GLA_DK * nh
    vv = GLA_DV * nh
    return pl.pallas_call(
        _gla_prompt_kernel,
        grid=(GLA_HEADS // nh, t // rows),
        in_specs=[
            pl.BlockSpec((rows, qk), lambda h, i: (i, Z_GQ // qk + h)),
            pl.BlockSpec((rows, qk), lambda h, i: (i, Z_GK // qk + h)),
            pl.BlockSpec((rows, vv), lambda h, i: (i, Z_GV // vv + h)),
            pl.BlockSpec((rows, vv), lambda h, i: (i, Z_GR // vv + h)),
            pl.BlockSpec((rows, qk), lambda h, i: (i, h)),
            pl.BlockSpec((1, vv), lambda h, i: (0, h)),
        ],
        out_specs=[
            pl.BlockSpec((rows, vv), lambda h, i: (i, h)),
            pl.BlockSpec((nh, GLA_DK, GLA_DV), lambda h, i: (h, 0, 0)),
        ],
        out_shape=[
            jax.ShapeDtypeStruct((t, GLA_WIDTH), F32),
            jax.ShapeDtypeStruct((GLA_HEADS, GLA_DK, GLA_DV), F32),
        ],
        scratch_shapes=[pltpu.VMEM((nh, GLA_DK, GLA_DV), F32)],
        compiler_params=_cparams(("parallel", "arbitrary")),
        name="gla_prompt",
    )(z, z, z, z, la, gn)


def _gla_sample_kernel(q_ref, k_ref, v_ref, r_ref, la_ref, gn_ref, s0_ref, y_ref, sfin_ref):
    rows = q_ref.shape[0]
    n_seq = rows // SAMPLE_T
    q = q_ref[...] * (GLA_DK ** -0.5)
    k = k_ref[...]
    v = v_ref[...].astype(BF16)
    la = la_ref[...]

    row = lax.broadcasted_iota(jnp.int32, (rows, rows), 0)
    col = lax.broadcasted_iota(jnp.int32, (rows, rows), 1)
    same_seq = (row // SAMPLE_T) == (col // SAMPLE_T)
    tri = ((col <= row) & same_seq).astype(BF16)
    b = _select_sum(tri, la)
    b_last = _group_row_bcast(b, SAMPLE_T, SAMPLE_T - 1)

    a = _gla_intra(q, k, b, SAMPLE_T, SAMPLE_T)
    o = _dot(a.astype(BF16), v)

    qd = q * jnp.exp(b)
    kd_t = (k * jnp.exp(b_last - b)).T
    la_t = la.T
    row1 = lax.broadcasted_iota(jnp.int32, (rows, 1), 0)
    lane1 = lax.broadcasted_iota(jnp.int32, (1, rows), 1)
    for sq in range(n_seq):
        s0 = s0_ref[sq, 0]
        in_rows = (row1 // SAMPLE_T) == sq
        in_lanes = (lane1 // SAMPLE_T) == sq
        o = o + _dot(jnp.where(in_rows, qd, 0.0).astype(BF16), s0.astype(BF16))
        decay = jnp.exp(jnp.sum(jnp.where(in_lanes, la_t, 0.0), axis=1, keepdims=True))
        sfin_ref[sq, 0] = decay * s0 + _dot(jnp.where(in_lanes, kd_t, 0.0).astype(BF16), v)
    y_ref[...] = _gla_head_out(o, gn_ref[...], r_ref[...])


def _gla_sample(z, la, gn, state):
    m = z.shape[0]
    rows = GLA_ROWS
    n_seq = rows // SAMPLE_T
    qk = GLA_DK
    vv = GLA_DV
    return pl.pallas_call(
        _gla_sample_kernel,
        grid=(GLA_HEADS, m // rows),
        in_specs=[
            pl.BlockSpec((rows, qk), lambda h, i: (i, Z_GQ // qk + h)),
            pl.BlockSpec((rows, qk), lambda h, i: (i, Z_GK // qk + h)),
            pl.BlockSpec((rows, vv), lambda h, i: (i, Z_GV // vv + h)),
            pl.BlockSpec((rows, vv), lambda h, i: (i, Z_GR // vv + h)),
            pl.BlockSpec((rows, qk), lambda h, i: (i, h)),
            pl.BlockSpec((1, vv), lambda h, i: (0, h)),
            pl.BlockSpec((n_seq, 1, qk, vv), lambda h, i: (i, h, 0, 0)),
        ],
        out_specs=[
            pl.BlockSpec((rows, vv), lambda h, i: (i, h)),
            pl.BlockSpec((n_seq, 1, qk, vv), lambda h, i: (i, h, 0, 0)),
        ],
        out_shape=[
            jax.ShapeDtypeStruct((m, GLA_WIDTH), F32),
            jax.ShapeDtypeStruct(state.shape, F32),
        ],
        compiler_params=_cparams(("parallel", "parallel")),
        name="gla_sample",
    )(z, z, z, z, la, gn, state)


def _lambda_value(lam_ref, lam_init):
    l = lam_ref[...]
    s1 = jnp.sum(l[0:1] * l[1:2], axis=-1, keepdims=True)
    s2 = jnp.sum(l[2:3] * l[3:4], axis=-1, keepdims=True)
    return jnp.exp(s1) - jnp.exp(s2) + lam_init


def _diff_head_out(acc1, l1, acc2, l2, lam, gn, lam_init):
    o = acc1 / l1 - lam * (acc2 / l2)
    o = o * lax.rsqrt(jnp.mean(o * o, axis=-1, keepdims=True) + EPS)
    return o * gn * (1.0 - lam_init)


def _diff_prompt_kernel(qi_ref, ki_ref, lam_ref, q_ref, k_ref, vt_ref, gn_ref, y_ref,
                        q_s, m_s, l_s, acc_s, *, lam_init):
    hg = pl.program_id(0)
    step_id = pl.program_id(1)
    qi = qi_ref[step_id]
    ki = ki_ref[step_id]
    bq = q_ref.shape[0]
    bk = k_ref.shape[0]
    w = 2 * DIFF_DK

    @pl.when(ki == 0)
    def _():
        q_s[...] = (q_ref[...] * (DIFF_DK ** -0.5 * LOG2_E)).astype(BF16)
        m_s[...] = jnp.full_like(m_s, NEG_BIG)
        l_s[...] = jnp.zeros_like(l_s)
        acc_s[...] = jnp.zeros_like(acc_s)

    def step(masked):
        kpos = (ki * bk - qi * bq + lax.broadcasted_iota(jnp.int32, (bk, V7X_LANES), 0)).astype(F32)
        if masked:
            row = lax.broadcasted_iota(jnp.int32, (bk, bq), 0)
            col = lax.broadcasted_iota(jnp.int32, (bk, bq), 1)
            keep = row <= col
        for hh in range(FLASH_HEADS):
            k = k_ref[:, hh * w:(hh + 1) * w].astype(BF16)
            vt = vt_ref[hh * DIFF_DV:(hh + 1) * DIFF_DV, :]
            head = jnp.full((bk, V7X_LANES), hg * FLASH_HEADS + hh + 1, jnp.int32).astype(F32)
            slope = jnp.exp2(-head * (8.0 / DIFF_HEADS)) * LOG2_E
            bias = jnp.tile(slope * kpos, (1, bq // V7X_LANES))
            for mp in range(2):
                c = hh * w + mp * DIFF_DK
                idx = hh * 2 + mp
                s = _dot_nt(k[:, mp * DIFF_DK:(mp + 1) * DIFF_DK], q_s[:, c:c + DIFF_DK]) + bias
                if masked:
                    s = jnp.where(keep, s, -jnp.inf)
                m_old = m_s[idx]
                m_new = jnp.maximum(m_old, jnp.max(s, axis=0, keepdims=True))
                alpha = jnp.exp2(m_old - m_new)
                p = jnp.exp2(s - m_new)
                l_s[idx] = alpha * l_s[idx] + jnp.sum(p, axis=0, keepdims=True)
                acc_s[idx] = alpha * acc_s[idx] + _dot(vt, p.astype(BF16))
                m_s[idx] = m_new

    @pl.when(ki < qi)
    def _():
        step(False)

    @pl.when(ki == qi)
    def _():
        step(True)
        lam = _lambda_value(lam_ref, lam_init)
        for hh in range(FLASH_HEADS):
            i1, i2 = hh * 2, hh * 2 + 1
            o = acc_s[i1] / l_s[i1] - lam * (acc_s[i2] / l_s[i2])
            o = o * lax.rsqrt(jnp.mean(o * o, axis=0, keepdims=True) + EPS)
            gn = gn_ref[hh * DIFF_DV:(hh + 1) * DIFF_DV, :]
            y_ref[:, hh * DIFF_DV:(hh + 1) * DIFF_DV] = (o * gn * (1.0 - lam_init)).T


def _diff_prompt(z, vt, lam_vecs, gn_col, lam_init):
    t = z.shape[0]
    bq = _pick_block(t, 512)
    w = 2 * DIFF_DK * FLASH_HEADS
    dvw = DIFF_DV * FLASH_HEADS
    nq = t // bq
    pairs = [(qi, ki) for qi in range(nq) for ki in range(qi + 1)]
    qi_arr = jnp.asarray([p[0] for p in pairs], jnp.int32)
    ki_arr = jnp.asarray([p[1] for p in pairs], jnp.int32)
    grid_spec = pltpu.PrefetchScalarGridSpec(
        num_scalar_prefetch=2,
        grid=(DIFF_HEADS // FLASH_HEADS, len(pairs)),
        in_specs=[
            pl.BlockSpec((4, DIFF_DK), lambda h, s, qi, ki: (0, 0)),
            pl.BlockSpec((bq, w), lambda h, s, qi, ki: (qi[s], Z_DQ // w + h)),
            pl.BlockSpec((bq, w), lambda h, s, qi, ki: (ki[s], Z_DK // w + h)),
            pl.BlockSpec((dvw, bq), lambda h, s, qi, ki: (h, ki[s])),
            pl.BlockSpec((dvw, 1), lambda h, s, qi, ki: (h, 0)),
        ],
        out_specs=pl.BlockSpec((bq, dvw), lambda h, s, qi, ki: (qi[s], h)),
        scratch_shapes=[
            pltpu.VMEM((bq, w), BF16),
            pltpu.VMEM((2 * FLASH_HEADS, 1, bq), F32),
            pltpu.VMEM((2 * FLASH_HEADS, 1, bq), F32),
            pltpu.VMEM((2 * FLASH_HEADS, DIFF_DV, bq), F32),
        ],
    )
    return pl.pallas_call(
        functools.partial(_diff_prompt_kernel, lam_init=lam_init),
        grid_spec=grid_spec,
        out_shape=jax.ShapeDtypeStruct((t, DIFF_WIDTH), F32),
        compiler_params=_cparams(("parallel", "arbitrary")),
        name="diff_prompt",
    )(qi_arr, ki_arr, lam_vecs, z, z, vt, gn_col)


def _diff_sample_kernel(pt_ref, lam_ref, q_ref, kn_ref, vn_ref, *rest, lam_init, past_len, pps):
    del pt_ref
    kp_refs, vp_refs = rest[:pps], rest[pps:2 * pps]
    gn_ref, y_ref, q_s, bias_s, m_s, l_s, acc_s = rest[2 * pps:]
    p = pl.program_id(1)
    tn = q_ref.shape[0]
    page = kp_refs[0].shape[2]
    hq = DIFF_HEADS * tn
    n_rows = 2 * hq
    n_keys = page * DIFF_HEADS
    w = 2 * DIFF_DK

    row1 = lax.broadcasted_iota(jnp.int32, (n_rows, 1), 0)
    slope = jnp.exp2(-((row1 % hq) // tn + 1).astype(F32) * (8.0 / DIFF_HEADS))

    def attend(kflat, vflat, bias):
        s = jnp.concatenate(
            [_dot_nt(q_s[mp], kflat[:, mp * DIFF_DK:(mp + 1) * DIFF_DK]) for mp in range(2)],
            axis=0) + bias
        m_old = m_s[...]
        m_new = jnp.maximum(m_old, jnp.max(s, axis=-1, keepdims=True))
        alpha = jnp.exp(m_old - m_new)
        pr = jnp.exp(s - m_new)
        l_s[...] = alpha * l_s[...] + jnp.sum(pr, axis=-1, keepdims=True)
        m_s[...] = m_new
        acc_s[...] = alpha * acc_s[...] + _dot(pr.astype(BF16), vflat)

    @pl.when((pl.program_id(0) == 0) & (p == 0))
    def _():
        row = lax.broadcasted_iota(jnp.int32, (n_rows, n_keys), 0)
        col = lax.broadcasted_iota(jnp.int32, (n_rows, n_keys), 1)
        same_head = (col % DIFF_HEADS) == ((row % hq) // tn)
        bias_s[...] = jnp.where(same_head, slope * (col // DIFF_HEADS).astype(F32), -jnp.inf)

    @pl.when(p == 0)
    def _():
        q = q_ref[...] * (DIFF_DK ** -0.5)
        for mp in range(2):
            q_s[mp] = jnp.concatenate(
                [q[:, hh * w + mp * DIFF_DK:hh * w + (mp + 1) * DIFF_DK] for hh in range(DIFF_HEADS)],
                axis=0).astype(BF16)
        m_s[...] = jnp.full_like(m_s, NEG_BIG)
        l_s[...] = jnp.zeros_like(l_s)
        acc_s[...] = jnp.zeros_like(acc_s)

        pad = jnp.zeros((V7X_LANES - hq, w), F32)
        k_new = jnp.concatenate(
            [kn_ref[:, hh * w:(hh + 1) * w] for hh in range(DIFF_HEADS)] + [pad], axis=0)
        v_new = jnp.concatenate(
            [vn_ref[:, hh * DIFF_DV:(hh + 1) * DIFF_DV] for hh in range(DIFF_HEADS)] + [pad], axis=0)
        rown = lax.broadcasted_iota(jnp.int32, (n_rows, V7X_LANES), 0)
        coln = lax.broadcasted_iota(jnp.int32, (n_rows, V7X_LANES), 1)
        visible = ((coln // tn) == ((rown % hq) // tn)) & ((coln % tn) <= (rown % tn))
        bias_new = jnp.where(visible, slope * (coln % tn).astype(F32), -jnp.inf)
        attend(k_new.astype(BF16), v_new.astype(BF16), bias_new)

    for j in range(pps):
        base = ((p * pps + j) * page - past_len).astype(F32)
        attend(kp_refs[j][0, 0].reshape(n_keys, w).astype(BF16),
               vp_refs[j][0, 0].reshape(n_keys, DIFF_DV).astype(BF16),
               bias_s[...] + slope * base)

    @pl.when(p == pl.num_programs(1) - 1)
    def _():
        lam = _lambda_value(lam_ref, lam_init)
        acc = acc_s[...]
        l = l_s[...]
        for hh in range(DIFF_HEADS):
            r1 = hh * tn
            r2 = hq + hh * tn
            y_ref[:, hh * DIFF_DV:(hh + 1) * DIFF_DV] = _diff_head_out(
                acc[r1:r1 + tn], l[r1:r1 + tn], acc[r2:r2 + tn], l[r2:r2 + tn],
                lam, gn_ref[:, hh * DIFF_DV:(hh + 1) * DIFF_DV], lam_init)


def _diff_sample(z, cache_k, cache_v, layer, page_table, lam_vecs, gn, lam_init):
    m = z.shape[0]
    tn = SAMPLE_T
    n_batch = m // tn
    n_pages = page_table.shape[1]
    page = cache_k.shape[2]
    pps = next(c for c in (8, 4, 2, 1) if n_pages % c == 0)
    pt = page_table.reshape(-1)
    n_rows = DIFF_HEADS * 2 * tn
    w = 2 * DIFF_DK

    def page_spec(width, j):
        return pl.BlockSpec(
            (1, 1, page, DIFF_HEADS, width),
            lambda b, p, pt: (layer, pt[b * n_pages + p * pps + j], 0, 0, 0))

    grid_spec = pltpu.PrefetchScalarGridSpec(
        num_scalar_prefetch=1,
        grid=(n_batch, n_pages // pps),
        in_specs=[
            pl.BlockSpec((4, DIFF_DK), lambda b, p, pt: (0, 0)),
            pl.BlockSpec((tn, DIFF_QK), lambda b, p, pt: (b, Z_DQ // DIFF_QK)),
            pl.BlockSpec((tn, DIFF_QK), lambda b, p, pt: (b, Z_DK // DIFF_QK)),
            pl.BlockSpec((tn, DIFF_WIDTH), lambda b, p, pt: (b, Z_DV // DIFF_WIDTH)),
        ] + [page_spec(w, j) for j in range(pps)] + [page_spec(DIFF_DV, j) for j in range(pps)] + [
            pl.BlockSpec((1, DIFF_WIDTH), lambda b, p, pt: (0, 0)),
        ],
        out_specs=pl.BlockSpec((tn, DIFF_WIDTH), lambda b, p, pt: (b, 0)),
        scratch_shapes=[
            pltpu.VMEM((2, DIFF_HEADS * tn, DIFF_DK), BF16),
            pltpu.VMEM((n_rows, page * DIFF_HEADS), F32),
            pltpu.VMEM((n_rows, 1), F32),
            pltpu.VMEM((n_rows, 1), F32),
            pltpu.VMEM((n_rows, DIFF_DV), F32),
        ],
    )
    return pl.pallas_call(
        functools.partial(_diff_sample_kernel, lam_init=lam_init, past_len=n_pages * page, pps=pps),
        grid_spec=grid_spec,
        out_shape=jax.ShapeDtypeStruct((m, DIFF_WIDTH), F32),
        compiler_params=_cparams(("arbitrary", "arbitrary")),
        name="diff_sample",
    )(pt, lam_vecs, z, z, z, *([cache_k] * pps), *([cache_v] * pps), gn)


def _prep_weights(w_in, w_alpha2, b_alpha, w_proj_a, w_proj_b, w_o, w_up, w_down):
    w_t = w_in.T
    w_main_t = jnp.concatenate(
        [w_t[:LR_START].astype(BF16), w_t[LR_START + GLA_RANK:].astype(BF16)], axis=0)
    w_lr_t = jnp.pad(w_t[LR_START:LR_START + GLA_RANK],
                     ((0, V7X_LANES - GLA_RANK), (0, 0))).astype(BF16)
    w_a2 = jnp.pad(w_alpha2, ((0, V7X_LANES - GLA_RANK), (0, 0))).astype(BF16)
    return dict(w_main_t=w_main_t, w_lr_t=w_lr_t, w_a2=w_a2, b_a=b_alpha.reshape(1, GLA_KEY),
                w_pa=w_proj_a.astype(BF16), w_pb=w_proj_b.astype(BF16), w_o=w_o.astype(BF16),
                w_up=w_up.astype(BF16), w_down=w_down.astype(BF16))


def _mix_inputs(x, norm_g, wts):
    h, la = _norm_lr(x, norm_g.reshape(1, D_MODEL), wts["w_lr_t"], wts["w_a2"], wts["b_a"])
    z = _in_proj(h, wts["w_main_t"])
    return z, la, h


def _channel_mix(x, ya, yb, z, wts, norm_mlp, fin, last):
    u = _merge(ya, yb, wts["w_pa"], wts["w_pb"], z)
    x2 = _oproj(u, wts["w_o"], x)
    return _mlp(x2, norm_mlp.reshape(1, D_MODEL), fin, wts["w_up"], wts["w_down"], last)


def kernel(x_prompt, x_sample, cache_k, cache_v, state_gla, page_table, norm_mix, w_in, w_alpha2,
           b_alpha, gla_norm, lambda_q1, lambda_k1, lambda_q2, lambda_k2, diff_norm, w_proj_a,
           w_proj_b, w_o, norm_mlp, w_up, w_down, final_norm):
    depth = w_in.shape[0]
    bp, tp = x_prompt.shape[0], x_prompt.shape[1]
    bs, ts = x_sample.shape[0], x_sample.shape[1]
    assert bp == 1 and ts == SAMPLE_T
    xp = x_prompt.reshape(tp, D_MODEL)
    xs = x_sample.reshape(bs * ts, D_MODEL)
    fin = final_norm.reshape(1, D_MODEL)
    kp_l, vp_l, sp_l, ks_l, vs_l, ss_l = [], [], [], [], [], []
    for l in range(depth):
        lam_init = 0.8 - 0.6 * math.exp(-0.3 * l)
        wts = _prep_weights(w_in[l], w_alpha2[l], b_alpha[l], w_proj_a[l], w_proj_b[l], w_o[l],
                            w_up[l], w_down[l])
        lam_vecs = jnp.stack([lambda_q1[l], lambda_k1[l], lambda_q2[l], lambda_k2[l]]).astype(F32)
        gn_a = gla_norm[l].reshape(1, GLA_WIDTH)
        gn_b = diff_norm[l].reshape(1, DIFF_WIDTH)
        last = l == depth - 1

        z, la, h = _mix_inputs(xp, norm_mix[l], wts)
        ya, s_fin = _gla_prompt(z, la, gn_a)
        vt = _vt_proj(wts["w_main_t"], h)
        yb = _diff_prompt(z, vt, lam_vecs, diff_norm[l].reshape(DIFF_WIDTH, 1), lam_init)
        xp = _channel_mix(xp, ya, yb, z, wts, norm_mlp[l], fin, last)
        kp_l.append(z[:, Z_DK:Z_DK + DIFF_QK].reshape(bp, tp, DIFF_HEADS, 2 * DIFF_DK))
        vp_l.append(z[:, Z_DV:Z_DV + DIFF_WIDTH].reshape(bp, tp, DIFF_HEADS, DIFF_DV))
        sp_l.append(s_fin.reshape(bp, GLA_HEADS, GLA_DK, GLA_DV))

        z, la, _ = _mix_inputs(xs, norm_mix[l], wts)
        ya, s_fin = _gla_sample(z, la, gn_a, state_gla[l])
        yb = _diff_sample(z, cache_k, cache_v, l, page_table, lam_vecs, gn_b, lam_init)
        xs = _channel_mix(xs, ya, yb, z, wts, norm_mlp[l], fin, last)
        ks_l.append(z[:, Z_DK:Z_DK + DIFF_QK].reshape(bs, ts, DIFF_HEADS, 2 * DIFF_DK))
        vs_l.append(z[:, Z_DV:Z_DV + DIFF_WIDTH].reshape(bs, ts, DIFF_HEADS, DIFF_DV))
        ss_l.append(s_fin)

    return (xp.reshape(bp, tp, D_MODEL), xs.reshape(bs, ts, D_MODEL), jnp.stack(kp_l),
            jnp.stack(vp_l), jnp.stack(sp_l), jnp.stack(ks_l), jnp.stack(vs_l), jnp.stack(ss_l))
```
